```python
import jax
import jax.numpy as jnp
from jax import lax
import numpy as np

D_MODEL = 1024
BATCH = 8
SEQ = 4096
DEPTH = 2
DEC_BATCH = 8
DEC_SEQ = 8192
PAST_LEN = 128

GRID_W = 64
PLE_DIM = 256
EPS = 1e-6

CONV_WIDTH = 512
CONV_K = 3

NA_HEADS = 8
NA_HEAD_DIM = 64
NA_WIDTH = NA_HEADS * NA_HEAD_DIM
WIN_ROWS = 8
WIN_COLS = 16
COL_BLOCK = 16
COL_SPAN = 2 * COL_BLOCK

GLA_HEADS = 4
GLA_DK = 64
GLA_DV = 128
GLA_KEY_WIDTH = GLA_HEADS * GLA_DK
GLA_VAL_WIDTH = GLA_HEADS * GLA_DV
GLA_RANK = 16
GLA_GATE_NORM = 16.0
GLA_CHUNK = 64

N_BRANCH = 3
BRANCH_WIDTH = 512

N_EXPERTS = 16
EXPERT_FF = 2048
CAPACITY_FACTOR = 2

SPLIT_SIZES = (CONV_WIDTH, CONV_WIDTH, CONV_WIDTH,
               NA_WIDTH, NA_WIDTH, NA_WIDTH,
               GLA_KEY_WIDTH, GLA_KEY_WIDTH, GLA_VAL_WIDTH, GLA_VAL_WIDTH, 2 * GLA_RANK,
               N_BRANCH * D_MODEL)
SPLIT_POINTS = tuple(int(s) for s in np.cumsum(SPLIT_SIZES)[:-1])
IN_COLS = int(sum(SPLIT_SIZES))

kernel_name = "hybrid_bidir_encoder_ec_moe"


def rmsnorm(x, w):
    xf = x.astype(jnp.float32)
    y = xf * lax.rsqrt(jnp.mean(xf * xf, axis=-1, keepdims=True) + EPS)
    return (y * w.astype(jnp.float32)).astype(x.dtype)


def short_conv_mixer(gate_b, gate_c, xa, conv_w):
    u = gate_c * xa
    L = u.shape[1]
    pad = CONV_K // 2
    up = jnp.pad(u, ((0, 0), (pad, pad), (0, 0)))
    y = up[:, 0:L] * conv_w[0]
    for j in range(1, CONV_K):
        y = y + up[:, j:j + L] * conv_w[j]
    return gate_b * y


def neighbourhood_attention(q, k, v, rpb):
    bsz, L, H, dh = q.shape
    rows = L // GRID_W
    kr = min(WIN_ROWS, rows)
    n_cb = GRID_W // COL_BLOCK
    q_cols = np.arange(GRID_W).reshape(n_cb, COL_BLOCK)
    col_start = np.clip(q_cols - WIN_COLS // 2, 0, GRID_W - WIN_COLS)
    blk_start = np.clip(np.arange(n_cb) * COL_BLOCK - WIN_COLS // 2, 0, GRID_W - COL_SPAN)
    key_cols = blk_start[:, None] + np.arange(COL_SPAN)
    kc = key_cols[:, None, :]
    col_ok = (kc >= col_start[..., None]) & (kc < col_start[..., None] + WIN_COLS)
    dc_idx = np.clip(kc - q_cols[..., None] + WIN_COLS - 1, 0, 2 * WIN_COLS - 2)
    rpb_c = rpb[:, :, dc_idx]
    qg = q.reshape(bsz, rows, n_cb, COL_BLOCK, H, dh)
    kg = k.reshape(bsz, rows, GRID_W, H, dh)[:, :, key_cols]
    vg = v.reshape(bsz, rows, GRID_W, H, dh)[:, :, key_cols]
    scale = dh ** -0.5

    def one_row(args):
        r, q_r = args
        rs = jnp.clip(r - kr // 2, 0, rows - kr)
        k_w = lax.dynamic_slice_in_dim(kg, rs, kr, axis=1)
        v_w = lax.dynamic_slice_in_dim(vg, rs, kr, axis=1)
        s = jnp.einsum('bjqhd,bijkhd->bhjqik', q_r, k_w,
                       preferred_element_type=jnp.float32) * scale
        dr_idx = rs + jnp.arange(kr) - r + WIN_ROWS - 1
        bias = jnp.take(rpb_c, dr_idx, axis=1).transpose(0, 2, 3, 1, 4)
        s = s + bias[None].astype(jnp.float32)
        s = jnp.where(col_ok[None, None, :, :, None, :], s, -jnp.inf)
        s = s.reshape(bsz, H, n_cb, COL_BLOCK, kr * COL_SPAN)
        p = jax.nn.softmax(s, axis=-1).reshape(bsz, H, n_cb, COL_BLOCK, kr, COL_SPAN).astype(v.dtype)
        return jnp.einsum('bhjqik,bijkhd->bjqhd', p, v_w)

    out = lax.map(one_row, (jnp.arange(rows), jnp.moveaxis(qg, 1, 0)))
    return jnp.moveaxis(out, 0, 1).reshape(bsz, L, H * dh)


def gla_causal(q, k, v, log_a):
    bsz, H, L, dk = q.shape
    dv = v.shape[-1]
    c = GLA_CHUNK
    n = L // c
    q = q.reshape(bsz, H, n, c, dk)
    k = k.reshape(bsz, H, n, c, dk)
    v = v.reshape(bsz, H, n, c, dv)
    b = jnp.cumsum(log_a.reshape(bsz, H, n, c, dk), axis=3)
    b_last = b[:, :, :, -1:]
    q_dec = q * jnp.exp(b)
    k_dec = k * jnp.exp(-b)
    tri = np.tril(np.ones((c, c), dtype=bool))
    att = jnp.where(tri, jnp.einsum('bhncd,bhnsd->bhncs', q_dec, k_dec), 0.0)
    o_intra = jnp.einsum('bhncs,bhnsv->bhncv', att, v)
    chunk_kv = jnp.einsum('bhncd,bhncv->nbhdv', k * jnp.exp(b_last - b), v)
    chunk_decay = jnp.moveaxis(jnp.exp(b_last[:, :, :, 0]), 2, 0)

    def step(S, inp):
        kv, dec = inp
        return dec[..., None] * S + kv, S

    _, S_in = lax.scan(step, jnp.zeros((bsz, H, dk, dv), jnp.float32), (chunk_kv, chunk_decay))
    o_inter = jnp.einsum('bhncd,nbhdv->bhncv', q_dec, S_in)
    return (o_intra + o_inter).reshape(bsz, H, L, dv)


def gla_mixer(q, k, v, g, lowrank, gla_w2, gla_b, gla_norm):
    bsz, L, _ = q.shape

    def heads(t, dh):
        return t.reshape(bsz, L, GLA_HEADS, dh).transpose(0, 2, 1, 3).astype(jnp.float32)

    z_f, z_b = jnp.split(lowrank, 2, axis=-1)
    la_f = jax.nn.log_sigmoid((jnp.einsum('blr,rk->blk', z_f, gla_w2[0]) + gla_b[0]).astype(jnp.float32)) / GLA_GATE_NORM
    la_b = jax.nn.log_sigmoid((jnp.einsum('blr,rk->blk', z_b, gla_w2[1]) + gla_b[1]).astype(jnp.float32)) / GLA_GATE_NORM
    qh = heads(q, GLA_DK) * (GLA_DK ** -0.5)
    kh = heads(k, GLA_DK)
    vh = heads(v, GLA_DV)
    o_fwd = gla_causal(qh, kh, vh, heads(la_f, GLA_DK))
    flip = lambda t: jnp.flip(t, axis=2)
    o_bwd = flip(gla_causal(flip(qh), flip(kh), flip(vh), flip(heads(la_b, GLA_DK))))
    o = o_fwd + o_bwd
    o = o * lax.rsqrt(jnp.mean(o * o, axis=-1, keepdims=True) + EPS)
    o = o * gla_norm.astype(jnp.float32).reshape(GLA_HEADS, 1, GLA_DV)
    o = o.transpose(0, 2, 1, 3).reshape(bsz, L, GLA_VAL_WIDTH).astype(q.dtype)
    return jax.nn.silu(g) * o


def mixer_block(h, w_in, conv_w, na_rpb, gla_w2, gla_b, gla_norm, w_branch, w_out):
    bsz, L, _ = h.shape
    proj = jnp.einsum('bld,de->ble', h, w_in)
    (a_b, a_c, a_x, n_q, n_k, n_v, g_q, g_k, g_v, g_g, g_lr, gates) = jnp.split(proj, SPLIT_POINTS, axis=-1)
    y_a = short_conv_mixer(a_b, a_c, a_x, conv_w)
    na = lambda t: t.reshape(bsz, L, NA_HEADS, NA_HEAD_DIM)
    y_b = neighbourhood_attention(na(n_q), na(n_k), na(n_v), na_rpb)
    y_c = gla_mixer(g_q, g_k, g_v, g_g, g_lr, gla_w2, gla_b, gla_norm)
    g_a, g_b, g_c = jnp.split(gates, N_BRANCH, axis=-1)
    merged = (jax.nn.sigmoid(g_a) * jnp.einsum('blw,wd->bld', y_a, w_branch[0])
              + jax.nn.sigmoid(g_b) * jnp.einsum('blw,wd->bld', y_b, w_branch[1])
              + jax.nn.sigmoid(g_c) * jnp.einsum('blw,wd->bld', y_c, w_branch[2]))
    return jnp.einsum('bld,de->ble', merged, w_out)


def expert_choice_ffn(h, w_router, w_exp_gate, w_exp_up, w_exp_down):
    bsz, L, d = h.shape
    tokens = h.reshape(bsz * L, d)
    n_tok = bsz * L
    cap = CAPACITY_FACTOR * n_tok // N_EXPERTS
    logits = jnp.einsum('td,de->te', tokens, w_router).astype(jnp.float32)
    affinity = jax.nn.softmax(logits, axis=-1)
    gate_val, idx = lax.top_k(affinity.T, cap)

    def run_expert(args):
        idx_e, g_e, wg, wu, wd = args
        xe = tokens[idx_e]
        he = jax.nn.silu(xe @ wg) * (xe @ wu)
        return (he @ wd) * g_e[:, None].astype(h.dtype)

    out_e = lax.map(run_expert, (idx, gate_val, w_exp_gate, w_exp_up, w_exp_down))
    y = jnp.zeros_like(tokens).at[idx.reshape(-1)].add(out_e.reshape(-1, d))
    return y.reshape(bsz, L, d)


def encoder_trunk(x, p, norm_mix, w_in, conv_w, na_rpb, gla_w2, gla_b, gla_norm, w_branch, w_out,
                  norm_ffn, w_router, w_exp_gate, w_exp_up, w_exp_down,
                  norm_ple, w_ple_gate, w_ple_proj, final_norm):
    for i in range(DEPTH):
        h = rmsnorm(x, norm_mix[i])
        x = x + mixer_block(h, w_in[i], conv_w[i], na_rpb[i], gla_w2[i], gla_b[i], gla_norm[i],
                            w_branch[i], w_out[i])
        h = rmsnorm(x, norm_ffn[i])
        x = x + expert_choice_ffn(h, w_router[i], w_exp_gate[i], w_exp_up[i], w_exp_down[i])
        ple_gate = jax.nn.sigmoid(jnp.einsum('bld,de->ble', rmsnorm(x, norm_ple[i]), w_ple_gate[i]))
        x = x + ple_gate * jnp.einsum('blp,pd->bld', p[i], w_ple_proj[i])
    return rmsnorm(x, final_norm)


def setup_inputs(seed: int = 0) -> dict:
    key = jax.random.key(seed)
    ks = jax.random.split(key, 24)
    f32 = jnp.float32

    def nrm(k, shape, scale):
        return jax.random.normal(k, shape, f32) * scale

    def gain(k, shape):
        return 1.0 + 0.1 * jax.random.normal(k, shape, f32)

    return {
        "x_prompt": nrm(ks[0], (BATCH, SEQ, D_MODEL), 1.0),
        "x_sample": nrm(ks[1], (DEC_BATCH, DEC_SEQ, D_MODEL), 1.0),
        "p_prompt": nrm(ks[2], (DEPTH, BATCH, SEQ, PLE_DIM), 1.0),
        "p_sample": nrm(ks[3], (DEPTH, DEC_BATCH, DEC_SEQ, PLE_DIM), 1.0),
        "norm_mix": gain(ks[4], (DEPTH, D_MODEL)),
        "w_in": nrm(ks[5], (DEPTH, D_MODEL, IN_COLS), D_MODEL ** -0.5),
        "conv_w": nrm(ks[6], (DEPTH, CONV_K, CONV_WIDTH), CONV_K ** -0.5),
        "na_rpb": nrm(ks[7], (DEPTH, NA_HEADS, 2 * WIN_ROWS - 1, 2 * WIN_COLS - 1), 0.1),
        "gla_w2": nrm(ks[8], (DEPTH, 2, GLA_RANK, GLA_KEY_WIDTH), GLA_RANK ** -0.5),
        "gla_b": nrm(ks[9], (DEPTH, 2, GLA_KEY_WIDTH), 0.5),
        "gla_norm": gain(ks[10], (DEPTH, GLA_VAL_WIDTH)),
        "w_branch": nrm(ks[11], (DEPTH, N_BRANCH, BRANCH_WIDTH, D_MODEL), BRANCH_WIDTH ** -0.5),
        "w_out": nrm(ks[12], (DEPTH, D_MODEL, D_MODEL), D_MODEL ** -0.5),
        "norm_ffn": gain(ks[13], (DEPTH, D_MODEL)),
        "w_router": nrm(ks[14], (DEPTH, D_MODEL, N_EXPERTS), D_MODEL ** -0.5),
        "w_exp_gate": nrm(ks[15], (DEPTH, N_EXPERTS, D_MODEL, EXPERT_FF), D_MODEL ** -0.5),
        "w_exp_up": nrm(ks[16], (DEPTH, N_EXPERTS, D_MODEL, EXPERT_FF), D_MODEL ** -0.5),
        "w_exp_down": nrm(ks[17], (DEPTH, N_EXPERTS, EXPERT_FF, D_MODEL), EXPERT_FF ** -0.5),
        "norm_ple": gain(ks[18], (DEPTH, D_MODEL)),
        "w_ple_gate": nrm(ks[19], (DEPTH, D_MODEL, D_MODEL), D_MODEL ** -0.5),
        "w_ple_proj": nrm(ks[20], (DEPTH, PLE_DIM, D_MODEL), PLE_DIM ** -0.5),
        "final_norm": gain(ks[21], (D_MODEL,)),
    }


def reference(x_prompt, x_sample, p_prompt, p_sample, norm_mix, w_in, conv_w, na_rpb, gla_w2, gla_b,
              gla_norm, w_branch, w_out, norm_ffn, w_router, w_exp_gate, w_exp_up, w_exp_down,
              norm_ple, w_ple_gate, w_ple_proj, final_norm):
    y_prompt = encoder_trunk(x_prompt, p_prompt, norm_mix, w_in, conv_w, na_rpb, gla_w2, gla_b, gla_norm,
                             w_branch, w_out, norm_ffn, w_router, w_exp_gate, w_exp_up, w_exp_down,
                             norm_ple, w_ple_gate, w_ple_proj, final_norm)
    y_sample = encoder_trunk(x_sample, p_sample, norm_mix, w_in, conv_w, na_rpb, gla_w2, gla_b, gla_norm,
                             w_branch, w_out, norm_ffn, w_router, w_exp_gate, w_exp_up, w_exp_down,
                             norm_ple, w_ple_gate, w_ple_proj, final_norm)
    return (y_prompt, y_sample)
```

```python
import functools

import numpy as np
import jax
import jax.numpy as jnp
from jax import lax
from jax.experimental import pallas as pl
from jax.experimental.pallas import tpu as pltpu

F32 = jnp.float32
BF16 = jnp.bfloat16

D_MODEL = 1024
GRID_W = 64
PLE_DIM = 256
EPS = 1e-6
CONV_WIDTH = 512
CONV_K = 3
NA_HEADS = 8
NA_HEAD_DIM = 64
NA_WIDTH = NA_HEADS * NA_HEAD_DIM
WIN_ROWS = 8
WIN_COLS = 16
GLA_HEADS = 4
GLA_DK = 64
GLA_DV = 128
GLA_KEY_WIDTH = GLA_HEADS * GLA_DK
GLA_VAL_WIDTH = GLA_HEADS * GLA_DV
GLA_RANK = 16
GLA_GATE_NORM = 16.0
GLA_CHUNK = 64
N_BRANCH = 3
N_EXPERTS = 16
EXPERT_FF = 2048
CAPACITY_FACTOR = 2

V7X_VMEM_BYTES = 64 * 1024 * 1024
LANES = 128
BF16_SUBLANES = 16

PROJ_TN = 512
COL_GATES = 0
COL_AB = 3 * D_MODEL
COL_AC = COL_AB + CONV_WIDTH
COL_AX = COL_AC + CONV_WIDTH
COL_NQ = COL_AX + CONV_WIDTH
COL_NK = COL_NQ + NA_WIDTH
COL_NV = COL_NK + NA_WIDTH
COL_GQ = COL_NV + NA_WIDTH
COL_GK = COL_GQ + GLA_KEY_WIDTH
COL_GV = COL_GK + GLA_KEY_WIDTH
COL_GG = COL_GV + GLA_VAL_WIDTH
PROJ_COLS = COL_GG + GLA_VAL_WIDTH
N_GATE_BLOCKS = (3 * D_MODEL) // PROJ_TN

TM_INPROJ = 1024
NA_TILE_ROWS = 4
NA_TQ = NA_TILE_ROWS * GRID_W
GLA_TC = 512
TM_MERGE = 256
TM_FFN = 512
TM_PLE = 512

NEG_INF = -1e30


def _vmem_limit(nbytes):
    return int(min(V7X_VMEM_BYTES - 6 * 1024 * 1024, max(32 * 1024 * 1024, nbytes + 8 * 1024 * 1024)))


def _sigmoid(x):
    return 1.0 / (1.0 + jnp.exp(-x))


def _split3(a):
    hi = a.astype(BF16)
    r1 = a - hi.astype(F32)
    mid = r1.astype(BF16)
    lo = (r1 - mid.astype(F32)).astype(BF16)
    return hi, mid, lo


def _inproj_kernel(x_ref, nw_ref, w_ref, wlr_ref, proj_ref, lr_ref, h_scr):
    j = pl.program_id(1)

    @pl.when(j == 0)
    def _():
        x = x_ref[...]
        ms = jnp.mean(x * x, axis=-1, keepdims=True)
        hb = (x * lax.rsqrt(ms + EPS) * nw_ref[...]).astype(BF16)
        h_scr[...] = hb
        lr_ref[...] = jnp.dot(hb, wlr_ref[...], preferred_element_type=F32)

    acc = jnp.dot(h_scr[...], w_ref[...], preferred_element_type=F32)

    @pl.when(j < N_GATE_BLOCKS)
    def _():
        proj_ref[...] = _sigmoid(acc).astype(BF16)

    @pl.when(j >= N_GATE_BLOCKS)
    def _():
        proj_ref[...] = acc.astype(BF16)


def _inproj(x2d, norm_w, w_main, w_lr):
    T = x2d.shape[0]
    tm = min(TM_INPROJ, T)
    nb = PROJ_COLS // PROJ_TN
    vmem = 2 * (tm * D_MODEL * 4 + D_MODEL * PROJ_TN * 2 + D_MODEL * LANES * 2 + tm * PROJ_TN * 2
                + tm * LANES * 4) + tm * D_MODEL * 2 + 2 * tm * PROJ_TN * 4
    return pl.pallas_call(
        _inproj_kernel,
        grid=(T // tm, nb),
        in_specs=[
            pl.BlockSpec((tm, D_MODEL), lambda i, j: (i, 0)),
            pl.BlockSpec((1, D_MODEL), lambda i, j: (0, 0)),
            pl.BlockSpec((D_MODEL, PROJ_TN), lambda i, j: (0, j)),
            pl.BlockSpec((D_MODEL, LANES), lambda i, j: (0, 0)),
        ],
        out_specs=[
            pl.BlockSpec((tm, PROJ_TN), lambda i, j: (i, j)),
            pl.BlockSpec((tm, LANES), lambda i, j: (i, 0)),
        ],
        out_shape=[
            jax.ShapeDtypeStruct((T, PROJ_COLS), BF16),
            jax.ShapeDtypeStruct((T, LANES), F32),
        ],
        scratch_shapes=[pltpu.VMEM((tm, D_MODEL), BF16)],
        compiler_params=pltpu.CompilerParams(
            dimension_semantics=("arbitrary", "arbitrary"), vmem_limit_bytes=_vmem_limit(vmem)),
        name="inproj",
    )(x2d, norm_w, w_main, w_lr)


def _na_tables(rows):
    R = NA_TILE_ROWS
    J = rows // R
    kr = min(WIN_ROWS, rows)
    qr_rel = np.arange(NA_TQ) // GRID_W
    qc = np.arange(NA_TQ) % GRID_W
    kk = np.arange(3 * NA_TQ)
    kblk = kk // NA_TQ
    kwithin = kk % NA_TQ
    kc = kwithin % GRID_W
    col_start = np.clip(qc - WIN_COLS // 2, 0, GRID_W - WIN_COLS)
    col_ok = (kc[None, :] >= col_start[:, None]) & (kc[None, :] < col_start[:, None] + WIN_COLS)
    dc = np.clip(kc[None, :] - qc[:, None] + WIN_COLS - 1, 0, 2 * WIN_COLS - 2)
    types, keys, tile_type = [], {}, []
    for j in range(J):
        blocks = np.array([max(j - 1, 0), j, min(j + 1, J - 1)])
        blk_ok = np.array([j - 1 >= 0, True, j + 1 <= J - 1])
        krow = blocks[kblk] * R + kwithin // GRID_W
        qrow = j * R + qr_rel
        rs = np.clip(qrow - kr // 2, 0, rows - kr)
        row_ok = (krow[None, :] >= rs[:, None]) & (krow[None, :] < rs[:, None] + kr) & blk_ok[kblk][None, :]
        mask = row_ok & col_ok
        dr = np.clip(krow[None, :] - qrow[:, None] + WIN_ROWS - 1, 0, 2 * WIN_ROWS - 2)
        dr = np.where(mask, dr, 0)
        key = mask.tobytes() + dr.astype(np.int8).tobytes()
        if key not in keys:
            keys[key] = len(types)
            types.append((mask, dr, np.where(mask, dc, 0)))
        tile_type.append(keys[key])
    mask = np.stack([t[0] for t in types])
    dr = np.stack([t[1] for t in types]).astype(np.int32)
    dcs = np.stack([t[2] for t in types]).astype(np.int32)
    return mask, dr, dcs, np.array(tile_type, np.int32)


def _na_kernel(tid_ref, q_ref, kp_ref, kc_ref, kn_ref, vp_ref, vc_ref, vn_ref, bias_ref, o_ref):
    del tid_ref
    q = q_ref[...]
    k = jnp.concatenate([kp_ref[...], kc_ref[...], kn_ref[...]], axis=0)
    v = jnp.concatenate([vp_ref[...], vc_ref[...], vn_ref[...]], axis=0)
    scale = NA_HEAD_DIM ** -0.5
    outs = []
    for h in range(NA_HEADS):
        sl = slice(h * NA_HEAD_DIM, (h + 1) * NA_HEAD_DIM)
        s = lax.dot_general(q[:, sl], k[:, sl], (((1,), (1,)), ((), ())), preferred_element_type=F32)
        s = s * scale + bias_ref[0, h]
        m = jnp.max(s, axis=-1, keepdims=True)
        p = jnp.exp(s - m)
        l = jnp.sum(p, axis=-1, keepdims=True)
        o = jnp.dot(p.astype(BF16), v[:, sl], preferred_element_type=F32)
        outs.append(o * (1.0 / l))
    o_ref[...] = jnp.concatenate(outs, axis=-1).astype(BF16)


def _na_attention(proj, rpb, bsz, L):
    T = proj.shape[0]
    rows = L // GRID_W
    J = rows // NA_TILE_ROWS
    mask, dr, dc, tile_type = _na_tables(rows)
    bias = jnp.where(mask[None], rpb[:, dr, dc], NEG_INF)
    bias = jnp.transpose(bias, (1, 0, 2, 3)).astype(F32)
    qb, kb, vb = COL_NQ // NA_WIDTH, COL_NK // NA_WIDTH, COL_NV // NA_WIDTH

    def prev(b, j, tid):
        return b * J + jnp.maximum(j - 1, 0)

    def nxt(b, j, tid):
        return b * J + jnp.minimum(j + 1, J - 1)

    blk = (NA_TQ, NA_WIDTH)
    vmem = 2 * (8 * NA_TQ * NA_WIDTH * 2 + NA_HEADS * NA_TQ * 3 * NA_TQ * 4) + 8 * NA_TQ * 3 * NA_TQ * 4
    grid_spec = pltpu.PrefetchScalarGridSpec(
        num_scalar_prefetch=1,
        grid=(bsz, J),
        in_specs=[
            pl.BlockSpec(blk, lambda b, j, tid: (b * J + j, qb)),
            pl.BlockSpec(blk, lambda b, j, tid: (prev(b, j, tid), kb)),
            pl.BlockSpec(blk, lambda b, j, tid: (b * J + j, kb)),
            pl.BlockSpec(blk, lambda b, j, tid: (nxt(b, j, tid), kb)),
            pl.BlockSpec(blk, lambda b, j, tid: (prev(b, j, tid), vb)),
            pl.BlockSpec(blk, lambda b, j, tid: (b * J + j, vb)),
            pl.BlockSpec(blk, lambda b, j, tid: (nxt(b, j, tid), vb)),
            pl.BlockSpec((1, NA_HEADS, NA_TQ, 3 * NA_TQ), lambda b, j, tid: (tid[j], 0, 0, 0)),
        ],
        out_specs=pl.BlockSpec(blk, lambda b, j, tid: (b * J + j, 0)),
    )
    return pl.pallas_call(
        _na_kernel,
        grid_spec=grid_spec,
        out_shape=jax.ShapeDtypeStruct((T, NA_WIDTH), BF16),
        compiler_params=pltpu.CompilerParams(
            dimension_semantics=("arbitrary", "arbitrary"), vmem_limit_bytes=_vmem_limit(vmem)),
        name="na_attention",
    )(jnp.asarray(tile_type), proj, proj, proj, proj, proj, proj, proj, bias)


def _gla_kernel(gq_ref, gk_ref, gv_ref, gg_ref, lr_ref, w2_ref, gb_ref, gn_ref, o_ref, ofwd_scr, st_scr,
                *, nt):
    p = pl.program_id(1)
    j = pl.program_id(2)
    c = GLA_CHUNK
    n_chunks = GLA_TC // c

    @pl.when(j == 0)
    def _():
        st_scr[...] = jnp.zeros_like(st_scr)

    lane_head = lax.broadcasted_iota(jnp.int32, (1, GLA_KEY_WIDTH), 1) // GLA_DK
    ri = lax.broadcasted_iota(jnp.int32, (c, c), 0)
    ci = lax.broadcasted_iota(jnp.int32, (c, c), 1)

    def sweep(direction):
        tri = (ci <= ri) if direction == 0 else (ci >= ri)
        tri_b = tri.astype(BF16)
        w2 = w2_ref[direction]
        gb = gb_ref[direction]
        st = st_scr[...]
        order = range(n_chunks) if direction == 0 else range(n_chunks - 1, -1, -1)
        for ch in order:
            rs = slice(ch * c, (ch + 1) * c)
            z = lr_ref[rs, :].astype(BF16)
            xg = jnp.dot(z, w2, preferred_element_type=F32) + gb
            la = (jnp.minimum(xg, 0.0) - jnp.log1p(jnp.exp(-jnp.abs(xg)))) * (1.0 / GLA_GATE_NORM)
            hi, mid, lo = _split3(la)
            b = (jnp.dot(tri_b, hi, preferred_element_type=F32)
                 + jnp.dot(tri_b, mid, preferred_element_type=F32)
                 + jnp.dot(tri_b, lo, preferred_element_type=F32))
            b_last = b[c - 1:c, :] if direction == 0 else b[0:1, :]
            q = gq_ref[rs, :].astype(F32) * (GLA_DK ** -0.5)
            k = gk_ref[rs, :].astype(F32)
            v = gv_ref[rs, :]
            qd = q * jnp.exp(b)
            kd = (k * jnp.exp(-b)).astype(BF16)
            krem = (k * jnp.exp(b_last - b)).astype(BF16)
            st_b = st.astype(BF16)
            outs = []
            for h in range(GLA_HEADS):
                qm = jnp.where(lane_head == h, qd, 0.0).astype(BF16)
                att = lax.dot_general(qm, kd, (((1,), (1,)), ((), ())), preferred_element_type=F32)
                att = jnp.where(tri, att, 0.0).astype(BF16)
                o_h = jnp.dot(att, v[:, h * GLA_DV:(h + 1) * GLA_DV], preferred_element_type=F32)
                o_h = o_h + lax.dot_general(qm, st_b, (((1,), (1,)), ((), ())), preferred_element_type=F32)
                outs.append(o_h)
            o_chunk = jnp.concatenate(outs, axis=-1)
            kv = lax.dot_general(v, krem, (((0,), (0,)), ((), ())), preferred_element_type=F32)
            st = st * jnp.exp(b_last)
            for h in range(GLA_HEADS):
                st = st + jnp.where(lane_head == h, kv[h * GLA_DV:(h + 1) * GLA_DV, :], 0.0)
            yield ch, o_chunk
        st_scr[...] = st

    @pl.when(p == 0)
    def _():
        for ch, o_chunk in sweep(0):
            start = pl.multiple_of(j * GLA_TC + ch * c, c)
            ofwd_scr[pl.ds(start, c), :] = o_chunk

    @pl.when(p == 1)
    def _():
        t = nt - 1 - j
        gn = gn_ref[...]
        for ch, o_chunk in sweep(1):
            start = pl.multiple_of(t * GLA_TC + ch * c, c)
            o = ofwd_scr[pl.ds(start, c), :] + o_chunk
            normed = []
            for h in range(GLA_HEADS):
                oh = o[:, h * GLA_DV:(h + 1) * GLA_DV]
                ms = jnp.mean(oh * oh, axis=-1, keepdims=True)
                normed.append(oh * lax.rsqrt(ms + EPS))
            on = jnp.concatenate(normed, axis=-1) * gn
            g = gg_ref[ch * c:(ch + 1) * c, :].astype(F32)
            o_ref[ch * c:(ch + 1) * c, :] = (g * _sigmoid(g) * on).astype(BF16)


def _gla(proj, lr, gla_w2, gla_b, gla_norm, bsz, L):
    T = proj.shape[0]
    nt = L // GLA_TC
    w2p = jnp.zeros((2, LANES, GLA_KEY_WIDTH), F32)
    w2p = w2p.at[0, 0:GLA_RANK].set(gla_w2[0]).at[1, GLA_RANK:2 * GLA_RANK].set(gla_w2[1]).astype(BF16)
    gb = gla_b.reshape(2, 1, GLA_KEY_WIDTH).astype(F32)
    gn = gla_norm.reshape(1, GLA_VAL_WIDTH).astype(F32)

    def tile(b, p, j):
        return b * nt + jnp.where(p == 0, j, nt - 1 - j)

    def otile(b, p, j):
        return b * nt + jnp.where(p == 0, nt - 1, nt - 1 - j)

    kq, kk = COL_GQ // GLA_KEY_WIDTH, COL_GK // GLA_KEY_WIDTH
    kv, kg = COL_GV // GLA_VAL_WIDTH, COL_GG // GLA_VAL_WIDTH
    vmem = (2 * (2 * GLA_TC * GLA_KEY_WIDTH * 2 + 3 * GLA_TC * GLA_VAL_WIDTH * 2 + GLA_TC * LANES * 4)
            + L * GLA_VAL_WIDTH * 4 + GLA_DV * GLA_KEY_WIDTH * 4 + 4 * 1024 * 1024)
    return pl.pallas_call(
        functools.partial(_gla_kernel, nt=nt),
        grid=(bsz, 2, nt),
        in_specs=[
            pl.BlockSpec((GLA_TC, GLA_KEY_WIDTH), lambda b, p, j: (tile(b, p, j), kq)),
            pl.BlockSpec((GLA_TC, GLA_KEY_WIDTH), lambda b, p, j: (tile(b, p, j), kk)),
            pl.BlockSpec((GLA_TC, GLA_VAL_WIDTH), lambda b, p, j: (tile(b, p, j), kv)),
            pl.BlockSpec((GLA_TC, GLA_VAL_WIDTH), lambda b, p, j: (tile(b, p, j), kg)),
            pl.BlockSpec((GLA_TC, LANES), lambda b, p, j: (tile(b, p, j), 0)),
            pl.BlockSpec((2, LANES, GLA_KEY_WIDTH), lambda b, p, j: (0, 0, 0)),
            pl.BlockSpec((2, 1, GLA_KEY_WIDTH), lambda b, p, j: (0, 0, 0)),
            pl.BlockSpec((1, GLA_VAL_WIDTH), lambda b, p, j: (0, 0)),
        ],
        out_specs=pl.BlockSpec((GLA_TC, GLA_VAL_WIDTH), lambda b, p, j: (otile(b, p, j), 0)),
        out_shape=jax.ShapeDtypeStruct((T, GLA_VAL_WIDTH), BF16),
        scratch_shapes=[pltpu.VMEM((L, GLA_VAL_WIDTH), F32), pltpu.VMEM((GLA_DV, GLA_KEY_WIDTH), F32)],
        compiler_params=pltpu.CompilerParams(
            dimension_semantics=("arbitrary", "arbitrary", "arbitrary"), vmem_limit_bytes=_vmem_limit(vmem)),
        name="gla",
    )(proj, proj, proj, proj, lr, w2p, gb, gn)


def _merge_kernel(x_ref, ga_ref, gb_ref, gc_ref, ab_ref, ac_ref, ax_ref, acp_ref, axp_ref, acn_ref, axn_ref,
                  yb_ref, yc_ref, cw_ref, wb_ref, wo_ref, nf_ref, wr_ref, xo_ref, h_ref, aff_ref, *, tiles_per_seq):
    i = pl.program_id(0)
    tm = x_ref.shape[0]
    pos = i % tiles_per_seq
    has_prev = jnp.where(pos == 0, 0.0, 1.0)
    has_next = jnp.where(pos == tiles_per_seq - 1, 0.0, 1.0)
    u = ac_ref[...].astype(F32) * ax_ref[...].astype(F32)
    up = (acp_ref[...].astype(F32) * axp_ref[...].astype(F32))[BF16_SUBLANES - 1:BF16_SUBLANES, :] * has_prev
    un = (acn_ref[...].astype(F32) * axn_ref[...].astype(F32))[0:1, :] * has_next
    ridx = lax.broadcasted_iota(jnp.int32, (tm, 1), 0)
    u_prev = jnp.where(ridx == 0, up, pltpu.roll(u, 1, axis=0))
    u_next = jnp.where(ridx == tm - 1, un, pltpu.roll(u, tm - 1, axis=0))
    cw = cw_ref[...]
    y_a = ab_ref[...].astype(F32) * (u_prev * cw[0:1, :] + u * cw[1:2, :] + u_next * cw[2:3, :])
    m = ga_ref[...].astype(F32) * jnp.dot(y_a.astype(BF16), wb_ref[0], preferred_element_type=F32)
    m = m + gb_ref[...].astype(F32) * jnp.dot(yb_ref[...], wb_ref[1], preferred_element_type=F32)
    m = m + gc_ref[...].astype(F32) * jnp.dot(yc_ref[...], wb_ref[2], preferred_element_type=F32)
    x_new = x_ref[...] + jnp.dot(m.astype(BF16), wo_ref[...], preferred_element_type=F32)
    xo_ref[...] = x_new
    ms = jnp.mean(x_new * x_new, axis=-1, keepdims=True)
    h = x_new * lax.rsqrt(ms + EPS) * nf_ref[...]
    h_ref[...] = h.astype(BF16)
    h0, h1, h2 = _split3(h)
    w0, w1, w2 = wr_ref[0], wr_ref[1], wr_ref[2]
    logits = (jnp.dot(h0, w0, preferred_element_type=F32) + jnp.dot(h0, w1, preferred_element_type=F32)
              + jnp.dot(h1, w0, preferred_element_type=F32) + jnp.dot(h1, w1, preferred_element_type=F32)
              + jnp.dot(h0, w2, preferred_element_type=F32) + jnp.dot(h2, w0, preferred_element_type=F32))
    lane = lax.broadcasted_iota(jnp.int32, logits.shape, 1)
    logits = jnp.where(lane < N_EXPERTS, logits, NEG_INF)
    mx = jnp.max(logits, axis=-1, keepdims=True)
    e = jnp.exp(logits - mx)
    aff_ref[...] = e / jnp.sum(e, axis=-1, keepdims=True)


def _merge(x2d, proj, y_b, y_c, conv_w, wb, wo, norm_ffn, w_router, L):
    T = x2d.shape[0]
    tm = TM_MERGE
    tiles_per_seq = L // tm
    n16 = T // BF16_SUBLANES
    per16 = tm // BF16_SUBLANES
    wr = jnp.zeros((D_MODEL, LANES), F32).at[:, :N_EXPERTS].set(w_router)
    wr = jnp.stack(_split3(wr))
    cb = CONV_WIDTH
    vmem = 2 * (2 * tm * D_MODEL * 4 + 3 * tm * D_MODEL * 2 + 5 * tm * cb * 2 + 4 * BF16_SUBLANES * cb * 2
                + 3 * cb * D_MODEL * 2 + D_MODEL * D_MODEL * 2 + 3 * D_MODEL * LANES * 2
                + tm * D_MODEL * 2 + tm * LANES * 4) + 8 * tm * D_MODEL * 4
    tok = lambda width, col: pl.BlockSpec((tm, width), lambda i: (i, col // width))
    halo_p = lambda col: pl.BlockSpec((BF16_SUBLANES, cb), lambda i: (jnp.maximum(i * per16 - 1, 0), col // cb))
    halo_n = lambda col: pl.BlockSpec((BF16_SUBLANES, cb),
                                      lambda i: (jnp.minimum((i + 1) * per16, n16 - 1), col // cb))
    full = lambda shape: pl.BlockSpec(shape, lambda i: (0,) * len(shape))
    return pl.pallas_call(
        functools.partial(_merge_kernel, tiles_per_seq=tiles_per_seq),
        grid=(T // tm,),
        in_specs=[
            tok(D_MODEL, 0),
            tok(D_MODEL, COL_GATES), tok(D_MODEL, COL_GATES + D_MODEL), tok(D_MODEL, COL_GATES + 2 * D_MODEL),
            tok(cb, COL_AB), tok(cb, COL_AC), tok(cb, COL_AX),
            halo_p(COL_AC), halo_p(COL_AX), halo_n(COL_AC), halo_n(COL_AX),
            tok(NA_WIDTH, 0), tok(GLA_VAL_WIDTH, 0),
            full((CONV_K, cb)), full((N_BRANCH, cb, D_MODEL)), full((D_MODEL, D_MODEL)),
            full((1, D_MODEL)), full((3, D_MODEL, LANES)),
        ],
        out_specs=[tok(D_MODEL, 0), tok(D_MODEL, 0), tok(LANES, 0)],
        out_shape=[
            jax.ShapeDtypeStruct((T, D_MODEL), F32),
            jax.ShapeDtypeStruct((T, D_MODEL), BF16),
            jax.ShapeDtypeStruct((T, LANES), F32),
        ],
        compiler_params=pltpu.CompilerParams(
            dimension_semantics=("arbitrary",), vmem_limit_bytes=_vmem_limit(vmem)),
        name="merge",
    )(x2d, proj, proj, proj, proj, proj, proj, proj, proj, proj, proj, y_b, y_c,
      conv_w, wb, wo, norm_ffn, wr)


def _ffn_kernel(xe_ref, g_ref, wg_ref, wu_ref, wd_ref, o_ref):
    xe = xe_ref[0]
    half = EXPERT_FF // 2
    acc = None
    for f in range(2):
        fs = slice(f * half, (f + 1) * half)
        a = jnp.dot(xe, wg_ref[0, :, fs], preferred_element_type=F32)
        b = jnp.dot(xe, wu_ref[0, :, fs], preferred_element_type=F32)
        he = (a * _sigmoid(a) * b).astype(BF16)
        part = jnp.dot(he, wd_ref[0, fs, :], preferred_element_type=F32)
        acc = part if acc is None else acc + part
    o_ref[0] = acc * g_ref[0]


def _expert_ffn(xe, gate, wg, wu, wd):
    E, cap, _ = xe.shape
    tm = min(TM_FFN, cap)
    vmem = 2 * (3 * D_MODEL * EXPERT_FF * 2 + tm * D_MODEL * 2 + tm * LANES * 4 + tm * D_MODEL * 4) \
        + 3 * tm * EXPERT_FF * 4
    return pl.pallas_call(
        _ffn_kernel,
        grid=(E, cap // tm),
        in_specs=[
            pl.BlockSpec((1, tm, D_MODEL), lambda e, i: (e, i, 0)),
            pl.BlockSpec((1, tm, 1), lambda e, i: (e, i, 0)),
            pl.BlockSpec((1, D_MODEL, EXPERT_FF), lambda e, i: (e, 0, 0)),
            pl.BlockSpec((1, D_MODEL, EXPERT_FF), lambda e, i: (e, 0, 0)),
            pl.BlockSpec((1, EXPERT_FF, D_MODEL), lambda e, i: (e, 0, 0)),
        ],
        out_specs=pl.BlockSpec((1, tm, D_MODEL), lambda e, i: (e, i, 0)),
        out_shape=jax.ShapeDtypeStruct((E, cap, D_MODEL), F32),
        compiler_params=pltpu.CompilerParams(
            dimension_semantics=("arbitrary", "arbitrary"), vmem_limit_bytes=_vmem_limit(vmem)),
        name="expert_ffn",
    )(xe, gate, wg, wu, wd)


def _ple_kernel(x_ref, y_ref, p_ref, np_ref, wg_ref, wp_ref, fn_ref, o_ref, *, final):
    x = x_ref[...] + y_ref[...]
    ms = jnp.mean(x * x, axis=-1, keepdims=True)
    h = (x * lax.rsqrt(ms + EPS) * np_ref[...]).astype(BF16)
    gate = _sigmoid(jnp.dot(h, wg_ref[...], preferred_element_type=F32))
    pp = jnp.dot(p_ref[...].astype(BF16), wp_ref[...], preferred_element_type=F32)
    x = x + gate * pp
    if final:
        ms = jnp.mean(x * x, axis=-1, keepdims=True)
        x = x * lax.rsqrt(ms + EPS) * fn_ref[...]
    o_ref[...] = x


def _ple(x2d, y2d, p2d, norm_ple, wg, wp, final_norm, final):
    T = x2d.shape[0]
    tm = min(TM_PLE, T)
    vmem = 2 * (3 * tm * D_MODEL * 4 + tm * PLE_DIM * 4 + D_MODEL * D_MODEL * 2 + PLE_DIM * D_MODEL * 2) \
        + 4 * tm * D_MODEL * 4
    tok = lambda width: pl.BlockSpec((tm, width), lambda i: (i, 0))
    full = lambda shape: pl.BlockSpec(shape, lambda i: (0,) * len(shape))
    return pl.pallas_call(
        functools.partial(_ple_kernel, final=final),
        grid=(T // tm,),
        in_specs=[tok(D_MODEL), tok(D_MODEL), tok(PLE_DIM), full((1, D_MODEL)), full((D_MODEL, D_MODEL)),
                  full((PLE_DIM, D_MODEL)), full((1, D_MODEL))],
        out_specs=tok(D_MODEL),
        out_shape=jax.ShapeDtypeStruct((T, D_MODEL), F32),
        compiler_params=pltpu.CompilerParams(
            dimension_semantics=("arbitrary",), vmem_limit_bytes=_vmem_limit(vmem)),
        name="ple",
    )(x2d, y2d, p2d, norm_ple, wg, wp, final_norm)


def _prep_layer(i, norm_mix, w_in, conv_w, na_rpb, gla_w2, gla_b, gla_norm, w_branch, w_out, norm_ffn, w_router,
                w_exp_gate, w_exp_up, w_exp_down, norm_ple, w_ple_gate, w_ple_proj):
    w = w_in[i]
    sizes = (CONV_WIDTH,) * 3 + (NA_WIDTH,) * 3 + (GLA_KEY_WIDTH,) * 2 + (GLA_VAL_WIDTH,) * 2 + (2 * GLA_RANK,)
    offs = np.concatenate([[0], np.cumsum(sizes)])
    gates_off = int(offs[-1])
    lr_off = int(offs[10])
    w_main = jnp.concatenate([w[:, gates_off:], w[:, :lr_off]], axis=1).astype(BF16)
    w_lr = jnp.zeros((D_MODEL, LANES), F32).at[:, :2 * GLA_RANK].set(w[:, lr_off:gates_off]).astype(BF16)
    return dict(
        norm_mix=norm_mix[i].reshape(1, D_MODEL), w_main=w_main, w_lr=w_lr, conv_w=conv_w[i], na_rpb=na_rpb[i],
        gla_w2=gla_w2[i], gla_b=gla_b[i], gla_norm=gla_norm[i], w_branch=w_branch[i].astype(BF16),
        w_out=w_out[i].astype(BF16), norm_ffn=norm_ffn[i].reshape(1, D_MODEL), w_router=w_router[i],
        w_exp_gate=w_exp_gate[i].astype(BF16), w_exp_up=w_exp_up[i].astype(BF16),
        w_exp_down=w_exp_down[i].astype(BF16), norm_ple=norm_ple[i].reshape(1, D_MODEL),
        w_ple_gate=w_ple_gate[i].astype(BF16), w_ple_proj=w_ple_proj[i].astype(BF16))


def _moe(h_bf16, aff, lp):
    n_tok = h_bf16.shape[0]
    cap = CAPACITY_FACTOR * n_tok // N_EXPERTS
    gate_val, idx = lax.top_k(aff[:, :N_EXPERTS].T, cap)
    xe = h_bf16[idx]
    out = _expert_ffn(xe, gate_val[..., None], lp["w_exp_gate"], lp["w_exp_up"], lp["w_exp_down"])
    return jnp.zeros((n_tok, D_MODEL), F32).at[idx.reshape(-1)].add(out.reshape(-1, D_MODEL))


def _trunk(x, p, layers, final_norm):
    bsz, L, _ = x.shape
    T = bsz * L
    x2d = x.reshape(T, D_MODEL)
    fn = final_norm.reshape(1, D_MODEL)
    for i, lp in enumerate(layers):
        proj, lr = _inproj(x2d, lp["norm_mix"], lp["w_main"], lp["w_lr"])
        y_b = _na_attention(proj, lp["na_rpb"], bsz, L)
        y_c = _gla(proj, lr, lp["gla_w2"], lp["gla_b"], lp["gla_norm"], bsz, L)
        x2d, h, aff = _merge(x2d, proj, y_b, y_c, lp["conv_w"], lp["w_branch"], lp["w_out"], lp["norm_ffn"],
                             lp["w_router"], L)
        y = _moe(h, aff, lp)
        x2d = _ple(x2d, y, p[i].reshape(T, PLE_DIM), lp["norm_ple"], lp["w_ple_gate"], lp["w_ple_proj"], fn,
                   final=(i == len(layers) - 1))
    return x2d.reshape(bsz, L, D_MODEL)


def kernel(x_prompt, x_sample, p_prompt, p_sample, norm_mix, w_in, conv_w, na_rpb, gla_w2, gla_b, gla_norm, w_branch, w_out, norm_ffn, w_router, w_exp_gate, w_exp_up, w_exp_down, norm_ple, w_ple_gate, w_ple_proj, final_norm):
    depth = w_in.shape[0]
    layers = [_prep_layer(i, norm_mix, w_in, conv_w, na_rpb, gla_w2, gla_b, gla_norm, w_branch, w_out, norm_ffn,
                          w_router, w_exp_gate, w_exp_up, w_exp_down, norm_ple, w_ple_gate, w_ple_proj)
              for i in range(depth)]
    y_prompt = _trunk(x_prompt, p_prompt, layers, final_norm)
    y_sample = _trunk(x_sample, p_sample, layers, final_norm)
    return (y_prompt, y_sample)
```

```python
import functools

import numpy as np
import jax
import jax.numpy as jnp
from jax import lax
from jax.experimental import pallas as pl
from jax.experimental.pallas import tpu as pltpu

F32 = jnp.float32
BF16 = jnp.bfloat16

D_MODEL = 1024
GRID_W = 64
PLE_DIM = 256
EPS = 1e-6
CONV_WIDTH = 512
CONV_K = 3
NA_HEADS = 8
NA_HEAD_DIM = 64
NA_WIDTH = NA_HEADS * NA_HEAD_DIM
WIN_ROWS = 8
WIN_COLS = 16
GLA_HEADS = 4
GLA_DK = 64
GLA_DV = 128
GLA_KEY_WIDTH = GLA_HEADS * GLA_DK
GLA_VAL_WIDTH = GLA_HEADS * GLA_DV
GLA_RANK = 16
GLA_GATE_NORM = 16.0
GLA_CHUNK = 64
N_BRANCH = 3
N_EXPERTS = 16
EXPERT_FF = 2048
CAPACITY_FACTOR = 2

V7X_VMEM_BYTES = 64 * 1024 * 1024
LANES = 128
BF16_SUBLANES = 16

PROJ_TN = 512
COL_GATES = 0
COL_AB = 3 * D_MODEL
COL_AC = COL_AB + CONV_WIDTH
COL_AX = COL_AC + CONV_WIDTH
COL_NQ = COL_AX + CONV_WIDTH
COL_NK = COL_NQ + NA_WIDTH
COL_NV = COL_NK + NA_WIDTH
COL_GQ = COL_NV + NA_WIDTH
COL_GK = COL_GQ + GLA_KEY_WIDTH
COL_GV = COL_GK + GLA_KEY_WIDTH
COL_GG = COL_GV + GLA_VAL_WIDTH
PROJ_COLS = COL_GG + GLA_VAL_WIDTH
N_GATE_BLOCKS = (3 * D_MODEL) // PROJ_TN

TM_INPROJ = 1024
NA_TILE_ROWS = 4
NA_TQ = NA_TILE_ROWS * GRID_W
GLA_TC = 512
TM_MERGE = 256
TM_FFN = 256
TM_PLE = 512

NEG_INF = -1e30


def _vmem_limit(nbytes):
    return int(min(V7X_VMEM_BYTES - 6 * 1024 * 1024, max(32 * 1024 * 1024, nbytes + 8 * 1024 * 1024)))


def _sigmoid(x):
    return 1.0 / (1.0 + jnp.exp(-x))


def _split3(a):
    hi = a.astype(BF16)
    r1 = a - hi.astype(F32)
    mid = r1.astype(BF16)
    lo = (r1 - mid.astype(F32)).astype(BF16)
    return hi, mid, lo


def _inproj_kernel(x_ref, nw_ref, w_ref, wlr_ref, proj_ref, lr_ref, h_scr):
    j = pl.program_id(1)

    @pl.when(j == 0)
    def _():
        x = x_ref[...]
        ms = jnp.mean(x * x, axis=-1, keepdims=True)
        hb = (x * lax.rsqrt(ms + EPS) * nw_ref[...]).astype(BF16)
        h_scr[...] = hb
        lr_ref[...] = jnp.dot(hb, wlr_ref[...], preferred_element_type=F32)

    acc = jnp.dot(h_scr[...], w_ref[...], preferred_element_type=F32)

    @pl.when(j < N_GATE_BLOCKS)
    def _():
        proj_ref[...] = _sigmoid(acc).astype(BF16)

    @pl.when(j >= N_GATE_BLOCKS)
    def _():
        proj_ref[...] = acc.astype(BF16)


def _inproj(x2d, norm_w, w_main, w_lr):
    T = x2d.shape[0]
    tm = min(TM_INPROJ, T)
    nb = PROJ_COLS // PROJ_TN
    vmem = 2 * (tm * D_MODEL * 4 + D_MODEL * PROJ_TN * 2 + D_MODEL * LANES * 2 + tm * PROJ_TN * 2
                + tm * LANES * 4) + tm * D_MODEL * 2 + 2 * tm * PROJ_TN * 4
    return pl.pallas_call(
        _inproj_kernel,
        grid=(T // tm, nb),
        in_specs=[
            pl.BlockSpec((tm, D_MODEL), lambda i, j: (i, 0)),
            pl.BlockSpec((1, D_MODEL), lambda i, j: (0, 0)),
            pl.BlockSpec((D_MODEL, PROJ_TN), lambda i, j: (0, j)),
            pl.BlockSpec((D_MODEL, LANES), lambda i, j: (0, 0)),
        ],
        out_specs=[
            pl.BlockSpec((tm, PROJ_TN), lambda i, j: (i, j)),
            pl.BlockSpec((tm, LANES), lambda i, j: (i, 0)),
        ],
        out_shape=[
            jax.ShapeDtypeStruct((T, PROJ_COLS), BF16),
            jax.ShapeDtypeStruct((T, LANES), F32),
        ],
        scratch_shapes=[pltpu.VMEM((tm, D_MODEL), BF16)],
        compiler_params=pltpu.CompilerParams(
            dimension_semantics=("arbitrary", "arbitrary"), vmem_limit_bytes=_vmem_limit(vmem)),
        name="inproj",
    )(x2d, norm_w, w_main, w_lr)


def _na_tables(rows):
    R = NA_TILE_ROWS
    J = rows // R
    kr = min(WIN_ROWS, rows)
    qr_rel = np.arange(NA_TQ) // GRID_W
    qc = np.arange(NA_TQ) % GRID_W
    kk = np.arange(3 * NA_TQ)
    kblk = kk // NA_TQ
    kwithin = kk % NA_TQ
    kc = kwithin % GRID_W
    col_start = np.clip(qc - WIN_COLS // 2, 0, GRID_W - WIN_COLS)
    col_ok = (kc[None, :] >= col_start[:, None]) & (kc[None, :] < col_start[:, None] + WIN_COLS)
    dc = np.clip(kc[None, :] - qc[:, None] + WIN_COLS - 1, 0, 2 * WIN_COLS - 2)
    types, keys, tile_type = [], {}, []
    for j in range(J):
        blocks = np.array([max(j - 1, 0), j, min(j + 1, J - 1)])
        blk_ok = np.array([j - 1 >= 0, True, j + 1 <= J - 1])
        krow = blocks[kblk] * R + kwithin // GRID_W
        qrow = j * R + qr_rel
        rs = np.clip(qrow - kr // 2, 0, rows - kr)
        row_ok = (krow[None, :] >= rs[:, None]) & (krow[None, :] < rs[:, None] + kr) & blk_ok[kblk][None, :]
        mask = row_ok & col_ok
        dr = np.clip(krow[None, :] - qrow[:, None] + WIN_ROWS - 1, 0, 2 * WIN_ROWS - 2)
        dr = np.where(mask, dr, 0)
        key = mask.tobytes() + dr.astype(np.int8).tobytes()
        if key not in keys:
            keys[key] = len(types)
            types.append((mask, dr, np.where(mask, dc, 0)))
        tile_type.append(keys[key])
    mask = np.stack([t[0] for t in types])
    dr = np.stack([t[1] for t in types]).astype(np.int32)
    dcs = np.stack([t[2] for t in types]).astype(np.int32)
    return mask, dr, dcs, np.array(tile_type, np.int32)


def _na_kernel(tid_ref, q_ref, kp_ref, kc_ref, kn_ref, vp_ref, vc_ref, vn_ref, bias_ref, o_ref):
    del tid_ref
    q = q_ref[...]
    k = jnp.concatenate([kp_ref[...], kc_ref[...], kn_ref[...]], axis=0)
    v = jnp.concatenate([vp_ref[...], vc_ref[...], vn_ref[...]], axis=0)
    scale = NA_HEAD_DIM ** -0.5
    outs = []
    for h in range(NA_HEADS):
        sl = slice(h * NA_HEAD_DIM, (h + 1) * NA_HEAD_DIM)
        s = lax.dot_general(q[:, sl], k[:, sl], (((1,), (1,)), ((), ())), preferred_element_type=F32)
        s = s * scale + bias_ref[0, h]
        m = jnp.max(s, axis=-1, keepdims=True)
        p = jnp.exp(s - m)
        l = jnp.sum(p, axis=-1, keepdims=True)
        o = jnp.dot(p.astype(BF16), v[:, sl], preferred_element_type=F32)
        outs.append(o * (1.0 / l))
    o_ref[...] = jnp.concatenate(outs, axis=-1).astype(BF16)


def _na_attention(proj, rpb, bsz, L):
    T = proj.shape[0]
    rows = L // GRID_W
    J = rows // NA_TILE_ROWS
    mask, dr, dc, tile_type = _na_tables(rows)
    bias = jnp.where(mask[None], rpb[:, dr, dc], NEG_INF)
    bias = jnp.transpose(bias, (1, 0, 2, 3)).astype(F32)
    qb, kb, vb = COL_NQ // NA_WIDTH, COL_NK // NA_WIDTH, COL_NV // NA_WIDTH

    def prev(b, j, tid):
        return b * J + jnp.maximum(j - 1, 0)

    def nxt(b, j, tid):
        return b * J + jnp.minimum(j + 1, J - 1)

    blk = (NA_TQ, NA_WIDTH)
    vmem = 2 * (8 * NA_TQ * NA_WIDTH * 2 + NA_HEADS * NA_TQ * 3 * NA_TQ * 4) + 8 * NA_TQ * 3 * NA_TQ * 4
    grid_spec = pltpu.PrefetchScalarGridSpec(
        num_scalar_prefetch=1,
        grid=(bsz, J),
        in_specs=[
            pl.BlockSpec(blk, lambda b, j, tid: (b * J + j, qb)),
            pl.BlockSpec(blk, lambda b, j, tid: (prev(b, j, tid), kb)),
            pl.BlockSpec(blk, lambda b, j, tid: (b * J + j, kb)),
            pl.BlockSpec(blk, lambda b, j, tid: (nxt(b, j, tid), kb)),
            pl.BlockSpec(blk, lambda b, j, tid: (prev(b, j, tid), vb)),
            pl.BlockSpec(blk, lambda b, j, tid: (b * J + j, vb)),
            pl.BlockSpec(blk, lambda b, j, tid: (nxt(b, j, tid), vb)),
            pl.BlockSpec((1, NA_HEADS, NA_TQ, 3 * NA_TQ), lambda b, j, tid: (tid[j], 0, 0, 0)),
        ],
        out_specs=pl.BlockSpec(blk, lambda b, j, tid: (b * J + j, 0)),
    )
    return pl.pallas_call(
        _na_kernel,
        grid_spec=grid_spec,
        out_shape=jax.ShapeDtypeStruct((T, NA_WIDTH), BF16),
        compiler_params=pltpu.CompilerParams(
            dimension_semantics=("arbitrary", "arbitrary"), vmem_limit_bytes=_vmem_limit(vmem)),
        name="na_attention",
    )(jnp.asarray(tile_type), proj, proj, proj, proj, proj, proj, proj, bias)


def _gla_kernel(gq_ref, gk_ref, gv_ref, gg_ref, lr_ref, w2_ref, gb_ref, gn_ref, o_ref, ofwd_scr, st_scr,
                *, nt):
    p = pl.program_id(1)
    j = pl.program_id(2)
    c = GLA_CHUNK
    n_chunks = GLA_TC // c

    @pl.when(j == 0)
    def _():
        st_scr[...] = jnp.zeros_like(st_scr)

    lane_head = lax.broadcasted_iota(jnp.int32, (1, GLA_KEY_WIDTH), 1) // GLA_DK
    ri = lax.broadcasted_iota(jnp.int32, (c, c), 0)
    ci = lax.broadcasted_iota(jnp.int32, (c, c), 1)

    def sweep(direction):
        tri = (ci <= ri) if direction == 0 else (ci >= ri)
        tri_b = tri.astype(BF16)
        w2 = w2_ref[direction]
        gb = gb_ref[direction]
        st = st_scr[...]
        order = range(n_chunks) if direction == 0 else range(n_chunks - 1, -1, -1)
        for ch in order:
            rs = slice(ch * c, (ch + 1) * c)
            z = lr_ref[rs, :].astype(BF16)
            xg = jnp.dot(z, w2, preferred_element_type=F32) + gb
            la = (jnp.minimum(xg, 0.0) - jnp.log1p(jnp.exp(-jnp.abs(xg)))) * (1.0 / GLA_GATE_NORM)
            hi, mid, lo = _split3(la)
            b = (jnp.dot(tri_b, hi, preferred_element_type=F32)
                 + jnp.dot(tri_b, mid, preferred_element_type=F32)
                 + jnp.dot(tri_b, lo, preferred_element_type=F32))
            b_last = b[c - 1:c, :] if direction == 0 else b[0:1, :]
            q = gq_ref[rs, :].astype(F32) * (GLA_DK ** -0.5)
            k = gk_ref[rs, :].astype(F32)
            v = gv_ref[rs, :]
            qd = q * jnp.exp(b)
            kd = (k * jnp.exp(-b)).astype(BF16)
            krem = (k * jnp.exp(b_last - b)).astype(BF16)
            st_b = st.astype(BF16)
            outs = []
            for h in range(GLA_HEADS):
                qm = jnp.where(lane_head == h, qd, 0.0).astype(BF16)
                att = lax.dot_general(qm, kd, (((1,), (1,)), ((), ())), preferred_element_type=F32)
                att = jnp.where(tri, att, 0.0).astype(BF16)
                o_h = jnp.dot(att, v[:, h * GLA_DV:(h + 1) * GLA_DV], preferred_element_type=F32)
                o_h = o_h + lax.dot_general(qm, st_b, (((1,), (1,)), ((), ())), preferred_element_type=F32)
                outs.append(o_h)
            o_chunk = jnp.concatenate(outs, axis=-1)
            kv = lax.dot_general(v, krem, (((0,), (0,)), ((), ())), preferred_element_type=F32)
            st = st * jnp.exp(b_last)
            for h in range(GLA_HEADS):
                st = st + jnp.where(lane_head == h, kv[h * GLA_DV:(h + 1) * GLA_DV, :], 0.0)
            yield ch, o_chunk
        st_scr[...] = st

    @pl.when(p == 0)
    def _():
        for ch, o_chunk in sweep(0):
            start = pl.multiple_of(j * GLA_TC + ch * c, c)
            ofwd_scr[pl.ds(start, c), :] = o_chunk

    @pl.when(p == 1)
    def _():
        t = nt - 1 - j
        gn = gn_ref[...]
        for ch, o_chunk in sweep(1):
            start = pl.multiple_of(t * GLA_TC + ch * c, c)
            o = ofwd_scr[pl.ds(start, c), :] + o_chunk
            normed = []
            for h in range(GLA_HEADS):
                oh = o[:, h * GLA_DV:(h + 1) * GLA_DV]
                ms = jnp.mean(oh * oh, axis=-1, keepdims=True)
                normed.append(oh * lax.rsqrt(ms + EPS))
            on = jnp.concatenate(normed, axis=-1) * gn
            g = gg_ref[ch * c:(ch + 1) * c, :].astype(F32)
            o_ref[ch * c:(ch + 1) * c, :] = (g * _sigmoid(g) * on).astype(BF16)


def _gla(proj, lr, gla_w2, gla_b, gla_norm, bsz, L):
    T = proj.shape[0]
    nt = L // GLA_TC
    w2p = jnp.zeros((2, LANES, GLA_KEY_WIDTH), F32)
    w2p = w2p.at[0, 0:GLA_RANK].set(gla_w2[0]).at[1, GLA_RANK:2 * GLA_RANK].set(gla_w2[1]).astype(BF16)
    gb = gla_b.reshape(2, 1, GLA_KEY_WIDTH).astype(F32)
    gn = gla_norm.reshape(1, GLA_VAL_WIDTH).astype(F32)

    def tile(b, p, j):
        return b * nt + jnp.where(p == 0, j, nt - 1 - j)

    def otile(b, p, j):
        return b * nt + jnp.where(p == 0, nt - 1, nt - 1 - j)

    kq, kk = COL_GQ // GLA_KEY_WIDTH, COL_GK // GLA_KEY_WIDTH
    kv, kg = COL_GV // GLA_VAL_WIDTH, COL_GG // GLA_VAL_WIDTH
    vmem = (2 * (2 * GLA_TC * GLA_KEY_WIDTH * 2 + 3 * GLA_TC * GLA_VAL_WIDTH * 2 + GLA_TC * LANES * 4)
            + L * GLA_VAL_WIDTH * 4 + GLA_DV * GLA_KEY_WIDTH * 4 + 4 * 1024 * 1024)
    return pl.pallas_call(
        functools.partial(_gla_kernel, nt=nt),
        grid=(bsz, 2, nt),
        in_specs=[
            pl.BlockSpec((GLA_TC, GLA_KEY_WIDTH), lambda b, p, j: (tile(b, p, j), kq)),
            pl.BlockSpec((GLA_TC, GLA_KEY_WIDTH), lambda b, p, j: (tile(b, p, j), kk)),
            pl.BlockSpec((GLA_TC, GLA_VAL_WIDTH), lambda b, p, j: (tile(b, p, j), kv)),
            pl.BlockSpec((GLA_TC, GLA_VAL_WIDTH), lambda b, p, j: (tile(b, p, j), kg)),
            pl.BlockSpec((GLA_TC, LANES), lambda b, p, j: (tile(b, p, j), 0)),
            pl.BlockSpec((2, LANES, GLA_KEY_WIDTH), lambda b, p, j: (0, 0, 0)),
            pl.BlockSpec((2, 1, GLA_KEY_WIDTH), lambda b, p, j: (0, 0, 0)),
            pl.BlockSpec((1, GLA_VAL_WIDTH), lambda b, p, j: (0, 0)),
        ],
        out_specs=pl.BlockSpec((GLA_TC, GLA_VAL_WIDTH), lambda b, p, j: (otile(b, p, j), 0)),
        out_shape=jax.ShapeDtypeStruct((T, GLA_VAL_WIDTH), BF16),
        scratch_shapes=[pltpu.VMEM((L, GLA_VAL_WIDTH), F32), pltpu.VMEM((GLA_DV, GLA_KEY_WIDTH), F32)],
        compiler_params=pltpu.CompilerParams(
            dimension_semantics=("arbitrary", "arbitrary", "arbitrary"), vmem_limit_bytes=_vmem_limit(vmem)),
        name="gla",
    )(proj, proj, proj, proj, lr, w2p, gb, gn)


def _merge_kernel(x_ref, ga_ref, gb_ref, gc_ref, ab_ref, ac_ref, ax_ref, acp_ref, axp_ref, acn_ref, axn_ref,
                  yb_ref, yc_ref, cw_ref, wb_ref, wo_ref, nf_ref, wr_ref, xo_ref, h_ref, aff_ref, *, tiles_per_seq):
    i = pl.program_id(0)
    tm = x_ref.shape[0]
    pos = i % tiles_per_seq
    has_prev = jnp.where(pos == 0, 0.0, 1.0)
    has_next = jnp.where(pos == tiles_per_seq - 1, 0.0, 1.0)
    u = ac_ref[...].astype(F32) * ax_ref[...].astype(F32)
    up = (acp_ref[...].astype(F32) * axp_ref[...].astype(F32))[BF16_SUBLANES - 1:BF16_SUBLANES, :] * has_prev
    un = (acn_ref[...].astype(F32) * axn_ref[...].astype(F32))[0:1, :] * has_next
    ridx = lax.broadcasted_iota(jnp.int32, (tm, 1), 0)
    u_prev = jnp.where(ridx == 0, up, pltpu.roll(u, 1, axis=0))
    u_next = jnp.where(ridx == tm - 1, un, pltpu.roll(u, tm - 1, axis=0))
    cw = cw_ref[...]
    y_a = ab_ref[...].astype(F32) * (u_prev * cw[0:1, :] + u * cw[1:2, :] + u_next * cw[2:3, :])
    m = ga_ref[...].astype(F32) * jnp.dot(y_a.astype(BF16), wb_ref[0], preferred_element_type=F32)
    m = m + gb_ref[...].astype(F32) * jnp.dot(yb_ref[...], wb_ref[1], preferred_element_type=F32)
    m = m + gc_ref[...].astype(F32) * jnp.dot(yc_ref[...], wb_ref[2], preferred_element_type=F32)
    x_new = x_ref[...] + jnp.dot(m.astype(BF16), wo_ref[...], preferred_element_type=F32)
    xo_ref[...] = x_new
    ms = jnp.mean(x_new * x_new, axis=-1, keepdims=True)
    h = x_new * lax.rsqrt(ms + EPS) * nf_ref[...]
    bits = lax.bitcast_convert_type(h.astype(BF16).astype(F32), jnp.uint32)
    half = D_MODEL // 2
    h_ref[...] = (bits[:, :half] >> 16) | (bits[:, half:] & jnp.uint32(0xFFFF0000))
    h0, h1, h2 = _split3(h)
    w0, w1, w2 = wr_ref[0], wr_ref[1], wr_ref[2]
    logits = (jnp.dot(h0, w0, preferred_element_type=F32) + jnp.dot(h0, w1, preferred_element_type=F32)
              + jnp.dot(h1, w0, preferred_element_type=F32) + jnp.dot(h1, w1, preferred_element_type=F32)
              + jnp.dot(h0, w2, preferred_element_type=F32) + jnp.dot(h2, w0, preferred_element_type=F32))
    lane = lax.broadcasted_iota(jnp.int32, logits.shape, 1)
    logits = jnp.where(lane < N_EXPERTS, logits, NEG_INF)
    mx = jnp.max(logits, axis=-1, keepdims=True)
    e = jnp.exp(logits - mx)
    aff_ref[...] = e / jnp.sum(e, axis=-1, keepdims=True)


def _merge(x2d, proj, y_b, y_c, conv_w, wb, wo, norm_ffn, w_router, L):
    T = x2d.shape[0]
    tm = TM_MERGE
    tiles_per_seq = L // tm
    n16 = T // BF16_SUBLANES
    per16 = tm // BF16_SUBLANES
    wr = jnp.zeros((D_MODEL, LANES), F32).at[:, :N_EXPERTS].set(w_router)
    wr = jnp.stack(_split3(wr))
    cb = CONV_WIDTH
    vmem = 2 * (2 * tm * D_MODEL * 4 + 3 * tm * D_MODEL * 2 + 5 * tm * cb * 2 + 4 * BF16_SUBLANES * cb * 2
                + 3 * cb * D_MODEL * 2 + D_MODEL * D_MODEL * 2 + 3 * D_MODEL * LANES * 2
                + tm * D_MODEL * 2 + tm * LANES * 4) + 8 * tm * D_MODEL * 4
    tok = lambda width, col: pl.BlockSpec((tm, width), lambda i: (i, col // width))
    halo_p = lambda col: pl.BlockSpec((BF16_SUBLANES, cb), lambda i: (jnp.maximum(i * per16 - 1, 0), col // cb))
    halo_n = lambda col: pl.BlockSpec((BF16_SUBLANES, cb),
                                      lambda i: (jnp.minimum((i + 1) * per16, n16 - 1), col // cb))
    full = lambda shape: pl.BlockSpec(shape, lambda i: (0,) * len(shape))
    return pl.pallas_call(
        functools.partial(_merge_kernel, tiles_per_seq=tiles_per_seq),
        grid=(T // tm,),
        in_specs=[
            tok(D_MODEL, 0),
            tok(D_MODEL, COL_GATES), tok(D_MODEL, COL_GATES + D_MODEL), tok(D_MODEL, COL_GATES + 2 * D_MODEL),
            tok(cb, COL_AB), tok(cb, COL_AC), tok(cb, COL_AX),
            halo_p(COL_AC), halo_p(COL_AX), halo_n(COL_AC), halo_n(COL_AX),
            tok(NA_WIDTH, 0), tok(GLA_VAL_WIDTH, 0),
            full((CONV_K, cb)), full((N_BRANCH, cb, D_MODEL)), full((D_MODEL, D_MODEL)),
            full((1, D_MODEL)), full((3, D_MODEL, LANES)),
        ],
        out_specs=[tok(D_MODEL, 0), tok(D_MODEL // 2, 0), tok(LANES, 0)],
        out_shape=[
            jax.ShapeDtypeStruct((T, D_MODEL), F32),
            jax.ShapeDtypeStruct((T, D_MODEL // 2), jnp.uint32),
            jax.ShapeDtypeStruct((T, LANES), F32),
        ],
        compiler_params=pltpu.CompilerParams(
            dimension_semantics=("arbitrary",), vmem_limit_bytes=_vmem_limit(vmem)),
        name="merge",
    )(x2d, proj, proj, proj, proj, proj, proj, proj, proj, proj, proj, y_b, y_c,
      conv_w, wb, wo, norm_ffn, wr)


def _moe_kernel(idx_ref, idxn_ref, g_ref, wg_ref, wu_ref, wd_ref, hp_hbm, xin_hbm, x_hbm,
                hbuf, ybuf, obuf, sem_h, sem_y, sem_w, *, nt, n_steps):
    del xin_hbm
    tm = obuf.shape[1]
    s = pl.program_id(0)
    i = s % nt
    slot = s % 2
    nslot = 1 - slot
    first = i == 0
    has_next = s + 1 < n_steps

    def gather_rows(idx, src_hbm, buf, sem, sl):
        def body(r, c):
            t = idx[0, 0, r]
            pltpu.make_async_copy(src_hbm.at[pl.ds(t, 1)], buf.at[sl, pl.ds(r, 1)], sem.at[sl]).start()
            return c
        lax.fori_loop(0, tm, body, 0, unroll=8)

    def wait_rows(src_hbm, buf, sem, sl):
        pltpu.make_async_copy(src_hbm.at[pl.ds(0, tm)], buf.at[sl], sem.at[sl]).wait()

    def scatter_rows(idx, sl):
        def body(r, c):
            t = idx[0, 0, r]
            pltpu.make_async_copy(obuf.at[sl, pl.ds(r, 1)], x_hbm.at[pl.ds(t, 1)], sem_w.at[sl]).start()
            return c
        lax.fori_loop(0, tm, body, 0, unroll=8)

    def wait_writes(sl):
        pltpu.make_async_copy(obuf.at[sl], x_hbm.at[pl.ds(0, tm)], sem_w.at[sl]).wait()

    @pl.when(s == 0)
    def _():
        gather_rows(idx_ref, hp_hbm, hbuf, sem_h, slot)

    @pl.when(first)
    def _():
        @pl.when(s > 0)
        def _():
            wait_writes(nslot)
        gather_rows(idx_ref, x_hbm, ybuf, sem_y, slot)

    wait_rows(hp_hbm, hbuf, sem_h, slot)
    w = hbuf[slot]
    lo = lax.bitcast_convert_type(w << 16, F32)
    hi = lax.bitcast_convert_type(w & jnp.uint32(0xFFFF0000), F32)
    xe = jnp.concatenate([lo, hi], axis=1).astype(BF16)
    half = EXPERT_FF // 2
    acc = None
    for f in range(2):
        fs = slice(f * half, (f + 1) * half)
        a = jnp.dot(xe, wg_ref[0, :, fs], preferred_element_type=F32)
        b = jnp.dot(xe, wu_ref[0, :, fs], preferred_element_type=F32)
        he = (a * _sigmoid(a) * b).astype(BF16)
        part = jnp.dot(he, wd_ref[0, fs, :], preferred_element_type=F32)
        acc = part if acc is None else acc + part
    out = acc * g_ref[0]

    @pl.when(jnp.logical_not(first))
    def _():
        wait_writes(nslot)

    @pl.when(has_next)
    def _():
        gather_rows(idxn_ref, hp_hbm, hbuf, sem_h, nslot)

        @pl.when(i < nt - 1)
        def _():
            gather_rows(idxn_ref, x_hbm, ybuf, sem_y, nslot)

    wait_rows(x_hbm, ybuf, sem_y, slot)
    obuf[slot] = ybuf[slot] + out
    scatter_rows(idx_ref, slot)

    @pl.when(s == n_steps - 1)
    def _():
        wait_writes(slot)


def _moe_ffn(x2d, hp, idx, gate, wg, wu, wd):
    E, cap = idx.shape
    T = x2d.shape[0]
    tm = min(TM_FFN, cap)
    nt = cap // tm
    n_steps = E * nt
    half = D_MODEL // 2
    idx3 = idx.reshape(n_steps, 1, tm).astype(jnp.int32)
    g3 = gate.reshape(n_steps, tm, 1).astype(F32)
    vmem = (2 * (3 * D_MODEL * EXPERT_FF * 2 + tm * LANES * 4) + 2 * tm * half * 4 + 4 * tm * D_MODEL * 4
            + 3 * tm * EXPERT_FF * 4)
    smem = lambda f: pl.BlockSpec((1, 1, tm), f, memory_space=pltpu.SMEM)
    wspec = lambda shape: pl.BlockSpec((1,) + shape, lambda s: (s // nt, 0, 0))
    return pl.pallas_call(
        functools.partial(_moe_kernel, nt=nt, n_steps=n_steps),
        grid=(n_steps,),
        in_specs=[
            smem(lambda s: (s, 0, 0)),
            smem(lambda s: (jnp.minimum(s + 1, n_steps - 1), 0, 0)),
            pl.BlockSpec((1, tm, 1), lambda s: (s, 0, 0)),
            wspec((D_MODEL, EXPERT_FF)), wspec((D_MODEL, EXPERT_FF)), wspec((EXPERT_FF, D_MODEL)),
            pl.BlockSpec(memory_space=pl.ANY),
            pl.BlockSpec(memory_space=pl.ANY),
        ],
        out_specs=pl.BlockSpec(memory_space=pl.ANY),
        out_shape=jax.ShapeDtypeStruct((T, D_MODEL), F32),
        scratch_shapes=[
            pltpu.VMEM((2, tm, half), jnp.uint32),
            pltpu.VMEM((2, tm, D_MODEL), F32),
            pltpu.VMEM((2, tm, D_MODEL), F32),
            pltpu.SemaphoreType.DMA((2,)),
            pltpu.SemaphoreType.DMA((2,)),
            pltpu.SemaphoreType.DMA((2,)),
        ],
        input_output_aliases={7: 0},
        compiler_params=pltpu.CompilerParams(
            dimension_semantics=("arbitrary",), vmem_limit_bytes=_vmem_limit(vmem)),
        name="moe_ffn",
    )(idx3, idx3, g3, wg, wu, wd, hp, x2d)


ROUTE_SLOT_CHUNK = 1024


def _select_kernel(a_ref, sel_ref, *, cap):
    keys = lax.bitcast_convert_type(a_ref[...], jnp.int32)
    n_exp, n_tok = keys.shape
    capf = jnp.float32(cap)

    def count(pred):
        return jnp.sum(pred.astype(F32), axis=1, keepdims=True)

    def value_bit(b, lo):
        cand = lo | jnp.left_shift(jnp.int32(1), 30 - b)
        return jnp.where(count(keys >= cand) >= capf, cand, lo)

    thr = lax.fori_loop(0, 31, value_bit, jnp.zeros((n_exp, 1), jnp.int32))
    gt = keys > thr
    eq = keys == thr
    need = capf - count(gt)
    tok = lax.broadcasted_iota(jnp.int32, keys.shape, 1)
    nbits = int(n_tok).bit_length()

    def index_bit(b, m):
        cand = m | jnp.left_shift(jnp.int32(1), nbits - 1 - b)
        ok = (cand <= n_tok) & (count(eq & (tok < cand)) <= need)
        return jnp.where(ok, cand, m)

    m = lax.fori_loop(0, nbits, index_bit, jnp.zeros((n_exp, 1), jnp.int32))
    sel_ref[...] = (gt | (eq & (tok < m))).astype(F32)


def _compact_kernel(sel_ref, aff_ref, idx_ref, gate_ref, *, cap, n_tok):
    sel = sel_ref[0]
    nb = sel.shape[0]
    ch = min(ROUTE_SLOT_CHUNK, cap)
    nt_dims = (((1,), (1,)), ((), ()))
    r128 = lax.broadcasted_iota(jnp.int32, (LANES, LANES), 0)
    c128 = lax.broadcasted_iota(jnp.int32, (LANES, LANES), 1)
    rj = lax.broadcasted_iota(jnp.int32, (nb, nb), 0)
    cj = lax.broadcasted_iota(jnp.int32, (nb, nb), 1)
    selb = sel.astype(BF16)
    cin = jnp.dot(selb, (r128 <= c128).astype(BF16), preferred_element_type=F32)
    totb = jnp.broadcast_to(cin[:, LANES - 1:LANES], (nb, LANES)).astype(BF16)
    bex = jnp.dot((cj < rj).astype(BF16), totb, preferred_element_type=F32)
    tot_row = lax.dot_general(jnp.ones((8, LANES), BF16), selb, nt_dims, preferred_element_type=F32)
    binc_row = jnp.dot(tot_row.astype(BF16), (rj <= cj).astype(BF16), preferred_element_type=F32)[0:1]
    bex_hi = jnp.floor(bex * (1.0 / 256.0))
    bex_lo = bex - 256.0 * bex_hi
    a_hi, a_mid, a_lo = _split3(aff_ref[0])
    table = jnp.concatenate([cin.astype(BF16), bex_hi.astype(BF16), bex_lo.astype(BF16), a_hi, a_mid, a_lo], axis=1)
    lane_nb = lax.broadcasted_iota(jnp.int32, (ch, nb), 1).astype(F32)
    lane128 = lax.broadcasted_iota(jnp.int32, (ch, LANES), 1).astype(F32)
    for c in range(cap // ch):
        s = (lax.broadcasted_iota(jnp.int32, (ch, 1), 0) + c * ch).astype(F32)
        blk = jnp.sum((binc_row <= s).astype(F32), axis=1, keepdims=True)
        onehot = (lane_nb == blk).astype(BF16)
        g = jnp.dot(onehot, table, preferred_element_type=F32)
        cinrow = g[:, 0:LANES]
        before = g[:, LANES:2 * LANES] * 256.0 + g[:, 2 * LANES:3 * LANES]
        affrow = g[:, 3 * LANES:4 * LANES] + g[:, 4 * LANES:5 * LANES] + g[:, 5 * LANES:6 * LANES]
        off = jnp.sum((cinrow <= s - before).astype(F32), axis=1, keepdims=True)
        idx = jnp.minimum(blk * float(LANES) + off, float(n_tok - 1))
        gate = jnp.sum(jnp.where(lane128 == off, affrow, 0.0), axis=1, keepdims=True)
        idx_t = jnp.transpose(jnp.broadcast_to(idx, (ch, LANES)))[0:1]
        gate_t = jnp.transpose(jnp.broadcast_to(gate, (ch, LANES)))[0:1]
        idx_ref[0, :, c * ch:(c + 1) * ch] = idx_t.astype(jnp.int32)
        gate_ref[0, :, c * ch:(c + 1) * ch] = gate_t


def _route(aff, cap):
    n_tok = aff.shape[0]
    nb = n_tok // LANES
    a_t = aff[:, :N_EXPERTS].T
    sel = pl.pallas_call(
        functools.partial(_select_kernel, cap=cap),
        out_shape=jax.ShapeDtypeStruct((N_EXPERTS, n_tok), F32),
        compiler_params=pltpu.CompilerParams(vmem_limit_bytes=_vmem_limit(12 * N_EXPERTS * n_tok * 4)),
        name="route_select",
    )(a_t)
    ch = min(ROUTE_SLOT_CHUNK, cap)
    vmem = 2 * (2 * nb * LANES * 4 + 2 * cap * 4) + 2 * nb * nb * 4 + nb * 6 * LANES * 4 + ch * (nb + 12 * LANES) * 4
    blk = pl.BlockSpec((1, nb, LANES), lambda e: (e, 0, 0))
    oblk = pl.BlockSpec((1, 1, cap), lambda e: (e, 0, 0))
    idx, gate = pl.pallas_call(
        functools.partial(_compact_kernel, cap=cap, n_tok=n_tok),
        grid=(N_EXPERTS,),
        in_specs=[blk, blk],
        out_specs=[oblk, oblk],
        out_shape=[jax.ShapeDtypeStruct((N_EXPERTS, 1, cap), jnp.int32),
                   jax.ShapeDtypeStruct((N_EXPERTS, 1, cap), F32)],
        compiler_params=pltpu.CompilerParams(
            dimension_semantics=("arbitrary",), vmem_limit_bytes=_vmem_limit(vmem)),
        name="route_compact",
    )(sel.reshape(N_EXPERTS, nb, LANES), a_t.reshape(N_EXPERTS, nb, LANES))
    return idx.reshape(N_EXPERTS, cap), gate.reshape(N_EXPERTS, cap)


def _ple_kernel(x_ref, p_ref, np_ref, wg_ref, wp_ref, fn_ref, o_ref, *, final):
    x = x_ref[...]
    ms = jnp.mean(x * x, axis=-1, keepdims=True)
    h = (x * lax.rsqrt(ms + EPS) * np_ref[...]).astype(BF16)
    gate = _sigmoid(jnp.dot(h, wg_ref[...], preferred_element_type=F32))
    pp = jnp.dot(p_ref[...].astype(BF16), wp_ref[...], preferred_element_type=F32)
    x = x + gate * pp
    if final:
        ms = jnp.mean(x * x, axis=-1, keepdims=True)
        x = x * lax.rsqrt(ms + EPS) * fn_ref[...]
    o_ref[...] = x


def _ple(x2d, p2d, norm_ple, wg, wp, final_norm, final):
    T = x2d.shape[0]
    tm = min(TM_PLE, T)
    vmem = 2 * (2 * tm * D_MODEL * 4 + tm * PLE_DIM * 4 + D_MODEL * D_MODEL * 2 + PLE_DIM * D_MODEL * 2) \
        + 4 * tm * D_MODEL * 4
    tok = lambda width: pl.BlockSpec((tm, width), lambda i: (i, 0))
    full = lambda shape: pl.BlockSpec(shape, lambda i: (0,) * len(shape))
    return pl.pallas_call(
        functools.partial(_ple_kernel, final=final),
        grid=(T // tm,),
        in_specs=[tok(D_MODEL), tok(PLE_DIM), full((1, D_MODEL)), full((D_MODEL, D_MODEL)),
                  full((PLE_DIM, D_MODEL)), full((1, D_MODEL))],
        out_specs=tok(D_MODEL),
        out_shape=jax.ShapeDtypeStruct((T, D_MODEL), F32),
        compiler_params=pltpu.CompilerParams(
            dimension_semantics=("arbitrary",), vmem_limit_bytes=_vmem_limit(vmem)),
        name="ple",
    )(x2d, p2d, norm_ple, wg, wp, final_norm)


def _prep_layer(i, norm_mix, w_in, conv_w, na_rpb, gla_w2, gla_b, gla_norm, w_branch, w_out, norm_ffn, w_router,
                w_exp_gate, w_exp_up, w_exp_down, norm_ple, w_ple_gate, w_ple_proj):
    w = w_in[i]
    sizes = (CONV_WIDTH,) * 3 + (NA_WIDTH,) * 3 + (GLA_KEY_WIDTH,) * 2 + (GLA_VAL_WIDTH,) * 2 + (2 * GLA_RANK,)
    offs = np.concatenate([[0], np.cumsum(sizes)])
    gates_off = int(offs[-1])
    lr_off = int(offs[10])
    w_main = jnp.concatenate([w[:, gates_off:], w[:, :lr_off]], axis=1).astype(BF16)
    w_lr = jnp.zeros((D_MODEL, LANES), F32).at[:, :2 * GLA_RANK].set(w[:, lr_off:gates_off]).astype(BF16)
    return dict(
        norm_mix=norm_mix[i].reshape(1, D_MODEL), w_main=w_main, w_lr=w_lr, conv_w=conv_w[i], na_rpb=na_rpb[i],
        gla_w2=gla_w2[i], gla_b=gla_b[i], gla_norm=gla_norm[i], w_branch=w_branch[i].astype(BF16),
        w_out=w_out[i].astype(BF16), norm_ffn=norm_ffn[i].reshape(1, D_MODEL), w_router=w_router[i],
        w_exp_gate=w_exp_gate[i].astype(BF16), w_exp_up=w_exp_up[i].astype(BF16),
        w_exp_down=w_exp_down[i].astype(BF16), norm_ple=norm_ple[i].reshape(1, D_MODEL),
        w_ple_gate=w_ple_gate[i].astype(BF16), w_ple_proj=w_ple_proj[i].astype(BF16))


def _moe(x2d, hp, aff, lp):
    n_tok = x2d.shape[0]
    cap = CAPACITY_FACTOR * n_tok // N_EXPERTS
    idx, gate_val = _route(aff, cap)
    return _moe_ffn(x2d, hp, idx, gate_val, lp["w_exp_gate"], lp["w_exp_up"], lp["w_exp_down"])


def _trunk(x, p, layers, final_norm):
    bsz, L, _ = x.shape
    T = bsz * L
    x2d = x.reshape(T, D_MODEL)
    fn = final_norm.reshape(1, D_MODEL)
    for i, lp in enumerate(layers):
        proj, lr = _inproj(x2d, lp["norm_mix"], lp["w_main"], lp["w_lr"])
        y_b = _na_attention(proj, lp["na_rpb"], bsz, L)
        y_c = _gla(proj, lr, lp["gla_w2"], lp["gla_b"], lp["gla_norm"], bsz, L)
        x2d, h, aff = _merge(x2d, proj, y_b, y_c, lp["conv_w"], lp["w_branch"], lp["w_out"], lp["norm_ffn"],
                             lp["w_router"], L)
        x2d = _moe(x2d, h, aff, lp)
        x2d = _ple(x2d, p[i].reshape(T, PLE_DIM), lp["norm_ple"], lp["w_ple_gate"], lp["w_ple_proj"], fn,
                   final=(i == len(layers) - 1))
    return x2d.reshape(bsz, L, D_MODEL)


def kernel(x_prompt, x_sample, p_prompt, p_sample, norm_mix, w_in, conv_w, na_rpb, gla_w2, gla_b, gla_norm, w_branch, w_out, norm_ffn, w_router, w_exp_gate, w_exp_up, w_exp_down, norm_ple, w_ple_gate, w_ple_proj, final_norm):
    depth = w_in.shape[0]
    layers = [_prep_layer(i, norm_mix, w_in, conv_w, na_rpb, gla_w2, gla_b, gla_norm, w_branch, w_out, norm_ffn,
                          w_router, w_exp_gate, w_exp_up, w_exp_down, norm_ple, w_ple_gate, w_ple_proj)
              for i in range(depth)]
    y_prompt = _trunk(x_prompt, p_prompt, layers, final_norm)
    y_sample = _trunk(x_sample, p_sample, layers, final_norm)
    return (y_prompt, y_sample)
```

```python
import functools

import numpy as np
import jax
import jax.numpy as jnp
from jax import lax
from jax.experimental import pallas as pl
from jax.experimental.pallas import tpu as pltpu

F32 = jnp.float32
BF16 = jnp.bfloat16

D_MODEL = 1024
GRID_W = 64
PLE_DIM = 256
EPS = 1e-6
CONV_WIDTH = 512
CONV_K = 3
NA_HEADS = 8
NA_HEAD_DIM = 64
NA_WIDTH = NA_HEADS * NA_HEAD_DIM
WIN_ROWS = 8
WIN_COLS = 16
GLA_HEADS = 4
GLA_DK = 64
GLA_DV = 128
GLA_KEY_WIDTH = GLA_HEADS * GLA_DK
GLA_VAL_WIDTH = GLA_HEADS * GLA_DV
GLA_RANK = 16
GLA_GATE_NORM = 16.0
GLA_CHUNK = 64
N_BRANCH = 3
N_EXPERTS = 16
EXPERT_FF = 2048
CAPACITY_FACTOR = 2

V7X_VMEM_BYTES = 64 * 1024 * 1024
LANES = 128
BF16_SUBLANES = 16

PROJ_TN = 512
COL_GATES = 0
COL_AB = 3 * D_MODEL
COL_AC = COL_AB + CONV_WIDTH
COL_AX = COL_AC + CONV_WIDTH
COL_NQ = COL_AX + CONV_WIDTH
COL_NK = COL_NQ + NA_WIDTH
COL_NV = COL_NK + NA_WIDTH
COL_GQ = COL_NV + NA_WIDTH
COL_GK = COL_GQ + GLA_KEY_WIDTH
COL_GV = COL_GK + GLA_KEY_WIDTH
COL_GG = COL_GV + GLA_VAL_WIDTH
PROJ_COLS = COL_GG + GLA_VAL_WIDTH
N_GATE_BLOCKS = (3 * D_MODEL) // PROJ_TN

TM_INPROJ = 512
NA_TILE_ROWS = 4
NA_TQ = NA_TILE_ROWS * GRID_W
GLA_TC = 512
TM_MERGE = 256
TM_FFN = 256
TM_PLE = 256

NEG_INF = -1e30


def _vmem_limit(nbytes):
    return int(min(V7X_VMEM_BYTES - 6 * 1024 * 1024, max(32 * 1024 * 1024, nbytes + 8 * 1024 * 1024)))


def _sigmoid(x):
    return 1.0 / (1.0 + jnp.exp(-x))


def _split3(a):
    hi = a.astype(BF16)
    r1 = a - hi.astype(F32)
    mid = r1.astype(BF16)
    lo = (r1 - mid.astype(F32)).astype(BF16)
    return hi, mid, lo


def _inproj_kernel(x_ref, nw_ref, wlr_ref, w_hbm, proj_ref, lr_ref, w_vmem, sem):
    @pl.when(pl.program_id(0) == 0)
    def _():
        cp = pltpu.make_async_copy(w_hbm, w_vmem, sem)
        cp.start()
        cp.wait()

    x = x_ref[...]
    ms = jnp.mean(x * x, axis=-1, keepdims=True)
    hb = (x * lax.rsqrt(ms + EPS) * nw_ref[...]).astype(BF16)
    lr_ref[...] = jnp.dot(hb, wlr_ref[...], preferred_element_type=F32)
    for j in range(PROJ_COLS // PROJ_TN):
        cs = slice(j * PROJ_TN, (j + 1) * PROJ_TN)
        acc = jnp.dot(hb, w_vmem[:, cs], preferred_element_type=F32)
        if j < N_GATE_BLOCKS:
            acc = _sigmoid(acc)
        proj_ref[:, cs] = acc.astype(BF16)


def _inproj(x2d, norm_w, w_main, w_lr):
    T = x2d.shape[0]
    tm = min(TM_INPROJ, T)
    vmem = (2 * (tm * D_MODEL * 4 + D_MODEL * LANES * 2 + tm * PROJ_COLS * 2 + tm * LANES * 4)
            + D_MODEL * PROJ_COLS * 2 + tm * D_MODEL * 2 + 4 * tm * PROJ_TN * 4)
    return pl.pallas_call(
        _inproj_kernel,
        grid=(T // tm,),
        in_specs=[
            pl.BlockSpec((tm, D_MODEL), lambda i: (i, 0)),
            pl.BlockSpec((1, D_MODEL), lambda i: (0, 0)),
            pl.BlockSpec((D_MODEL, LANES), lambda i: (0, 0)),
            pl.BlockSpec(memory_space=pl.ANY),
        ],
        out_specs=[
            pl.BlockSpec((tm, PROJ_COLS), lambda i: (i, 0)),
            pl.BlockSpec((tm, LANES), lambda i: (i, 0)),
        ],
        out_shape=[
            jax.ShapeDtypeStruct((T, PROJ_COLS), BF16),
            jax.ShapeDtypeStruct((T, LANES), F32),
        ],
        scratch_shapes=[pltpu.VMEM((D_MODEL, PROJ_COLS), BF16), pltpu.SemaphoreType.DMA],
        compiler_params=pltpu.CompilerParams(
            dimension_semantics=("arbitrary",), vmem_limit_bytes=_vmem_limit(vmem)),
        name="inproj",
    )(x2d, norm_w, w_lr, w_main)


def _na_tables(rows):
    R = NA_TILE_ROWS
    J = rows // R
    kr = min(WIN_ROWS, rows)
    qr_rel = np.arange(NA_TQ) // GRID_W
    qc = np.arange(NA_TQ) % GRID_W
    kk = np.arange(3 * NA_TQ)
    kblk = kk // NA_TQ
    kwithin = kk % NA_TQ
    kc = kwithin % GRID_W
    col_start = np.clip(qc - WIN_COLS // 2, 0, GRID_W - WIN_COLS)
    col_ok = (kc[None, :] >= col_start[:, None]) & (kc[None, :] < col_start[:, None] + WIN_COLS)
    dc = np.clip(kc[None, :] - qc[:, None] + WIN_COLS - 1, 0, 2 * WIN_COLS - 2)
    types, keys, tile_type = [], {}, []
    for j in range(J):
        blocks = np.array([max(j - 1, 0), j, min(j + 1, J - 1)])
        blk_ok = np.array([j - 1 >= 0, True, j + 1 <= J - 1])
        krow = blocks[kblk] * R + kwithin // GRID_W
        qrow = j * R + qr_rel
        rs = np.clip(qrow - kr // 2, 0, rows - kr)
        row_ok = (krow[None, :] >= rs[:, None]) & (krow[None, :] < rs[:, None] + kr) & blk_ok[kblk][None, :]
        mask = row_ok & col_ok
        dr = np.clip(krow[None, :] - qrow[:, None] + WIN_ROWS - 1, 0, 2 * WIN_ROWS - 2)
        dr = np.where(mask, dr, 0)
        key = mask.tobytes() + dr.astype(np.int8).tobytes()
        if key not in keys:
            keys[key] = len(types)
            types.append((mask, dr, np.where(mask, dc, 0)))
        tile_type.append(keys[key])
    mask = np.stack([t[0] for t in types])
    dr = np.stack([t[1] for t in types]).astype(np.int32)
    dcs = np.stack([t[2] for t in types]).astype(np.int32)
    return mask, dr, dcs, np.array(tile_type, np.int32)


def _na_bias(rpb, mask, dr):
    n_types = mask.shape[0]
    W = GRID_W
    qc = np.arange(W)[:, None]
    kc = np.arange(W)[None, :]
    col_start = np.clip(qc - WIN_COLS // 2, 0, W - WIN_COLS)
    col_ok = (kc >= col_start) & (kc < col_start + WIN_COLS)
    n_dc = 2 * WIN_COLS - 1
    lead = W - WIN_COLS
    padded = jnp.pad(rpb, ((0, 0), (0, 0), (lead, 2 * W - 1 - lead - n_dc)), constant_values=NEG_INF)
    toep = jnp.stack([padded[:, :, W - 1 - q:2 * W - 1 - q] for q in range(W)], axis=2)
    toep = jnp.where(col_ok[None, None], toep, NEG_INF)
    dead = jnp.full((NA_HEADS, 1, W, W), NEG_INF, F32)
    blocks = jnp.concatenate([toep, dead], axis=1)
    n_dead = toep.shape[1]
    tables = []
    for t in range(n_types):
        ids = np.where(mask[t, ::W, ::W], dr[t, ::W, ::W], n_dead)
        rows_ = [jnp.concatenate([blocks[:, int(b)] for b in ids_row], axis=-1) for ids_row in ids]
        tables.append(jnp.concatenate(rows_, axis=1))
    return jnp.stack(tables)


def _na_kernel(tid_ref, q_ref, kp_ref, kc_ref, kn_ref, vp_ref, vc_ref, vn_ref, bias_ref, o_ref):
    del tid_ref
    q = q_ref[...]
    k = jnp.concatenate([kp_ref[...], kc_ref[...], kn_ref[...]], axis=0)
    v = jnp.concatenate([vp_ref[...], vc_ref[...], vn_ref[...]], axis=0)
    outs = []
    for h in range(NA_HEADS):
        sl = slice(h * NA_HEAD_DIM, (h + 1) * NA_HEAD_DIM)
        s = lax.dot_general(q[:, sl], k[:, sl], (((1,), (1,)), ((), ())), preferred_element_type=F32)
        s = s + bias_ref[0, h]
        m = jnp.max(s, axis=-1, keepdims=True)
        p = jnp.exp(s - m)
        l = jnp.sum(p, axis=-1, keepdims=True)
        o = jnp.dot(p.astype(BF16), v[:, sl], preferred_element_type=F32)
        outs.append(o * (1.0 / l))
    o_ref[...] = jnp.concatenate(outs, axis=-1).astype(BF16)


def _na_attention(proj, rpb, bsz, L):
    T = proj.shape[0]
    rows = L // GRID_W
    J = rows // NA_TILE_ROWS
    mask, dr, _, tile_type = _na_tables(rows)
    bias = _na_bias(rpb.astype(F32), mask, dr)
    qb, kb, vb = COL_NQ // NA_WIDTH, COL_NK // NA_WIDTH, COL_NV // NA_WIDTH

    def prev(b, j, tid):
        return b * J + jnp.maximum(j - 1, 0)

    def nxt(b, j, tid):
        return b * J + jnp.minimum(j + 1, J - 1)

    blk = (NA_TQ, NA_WIDTH)
    vmem = 2 * (8 * NA_TQ * NA_WIDTH * 2 + NA_HEADS * NA_TQ * 3 * NA_TQ * 4) + 8 * NA_TQ * 3 * NA_TQ * 4
    grid_spec = pltpu.PrefetchScalarGridSpec(
        num_scalar_prefetch=1,
        grid=(bsz, J),
        in_specs=[
            pl.BlockSpec(blk, lambda b, j, tid: (b * J + j, qb)),
            pl.BlockSpec(blk, lambda b, j, tid: (prev(b, j, tid), kb)),
            pl.BlockSpec(blk, lambda b, j, tid: (b * J + j, kb)),
            pl.BlockSpec(blk, lambda b, j, tid: (nxt(b, j, tid), kb)),
            pl.BlockSpec(blk, lambda b, j, tid: (prev(b, j, tid), vb)),
            pl.BlockSpec(blk, lambda b, j, tid: (b * J + j, vb)),
            pl.BlockSpec(blk, lambda b, j, tid: (nxt(b, j, tid), vb)),
            pl.BlockSpec((1, NA_HEADS, NA_TQ, 3 * NA_TQ), lambda b, j, tid: (tid[j], 0, 0, 0)),
        ],
        out_specs=pl.BlockSpec(blk, lambda b, j, tid: (b * J + j, 0)),
    )
    return pl.pallas_call(
        _na_kernel,
        grid_spec=grid_spec,
        out_shape=jax.ShapeDtypeStruct((T, NA_WIDTH), BF16),
        compiler_params=pltpu.CompilerParams(
            dimension_semantics=("arbitrary", "arbitrary"), vmem_limit_bytes=_vmem_limit(vmem)),
        name="na_attention",
    )(jnp.asarray(tile_type), proj, proj, proj, proj, proj, proj, proj, bias)


def _gla_kernel(gq_ref, gk_ref, gv_ref, gg_ref, lr_ref, w2_ref, gb_ref, gn_ref, o_ref, ofwd_scr, st_scr,
                *, nt):
    p = pl.program_id(1)
    j = pl.program_id(2)
    c = GLA_CHUNK
    n_chunks = GLA_TC // c

    @pl.when(j == 0)
    def _():
        st_scr[...] = jnp.zeros_like(st_scr)

    lane_head = lax.broadcasted_iota(jnp.int32, (1, GLA_KEY_WIDTH), 1) // GLA_DK
    ri = lax.broadcasted_iota(jnp.int32, (c, c), 0)
    ci = lax.broadcasted_iota(jnp.int32, (c, c), 1)

    def sweep(direction):
        tri = (ci <= ri) if direction == 0 else (ci >= ri)
        tri_b = tri.astype(BF16)
        w2 = w2_ref[direction]
        gb = gb_ref[direction]
        st = st_scr[...]
        order = range(n_chunks) if direction == 0 else range(n_chunks - 1, -1, -1)
        for ch in order:
            rs = slice(ch * c, (ch + 1) * c)
            z = lr_ref[rs, :].astype(BF16)
            xg = jnp.dot(z, w2, preferred_element_type=F32) + gb
            la = (jnp.minimum(xg, 0.0) - jnp.log1p(jnp.exp(-jnp.abs(xg)))) * (1.0 / GLA_GATE_NORM)
            hi, mid, lo = _split3(la)
            b = (jnp.dot(tri_b, hi, preferred_element_type=F32)
                 + jnp.dot(tri_b, mid, preferred_element_type=F32)
                 + jnp.dot(tri_b, lo, preferred_element_type=F32))
            b_last = b[c - 1:c, :] if direction == 0 else b[0:1, :]
            q = gq_ref[rs, :].astype(F32) * (GLA_DK ** -0.5)
            k = gk_ref[rs, :].astype(F32)
            v = gv_ref[rs, :]
            qd = q * jnp.exp(b)
            kd = (k * jnp.exp(-b)).astype(BF16)
            krem = (k * jnp.exp(b_last - b)).astype(BF16)
            st_b = st.astype(BF16)
            outs = []
            for h in range(GLA_HEADS):
                qm = jnp.where(lane_head == h, qd, 0.0).astype(BF16)
                att = lax.dot_general(qm, kd, (((1,), (1,)), ((), ())), preferred_element_type=F32)
                att = jnp.where(tri, att, 0.0).astype(BF16)
                o_h = jnp.dot(att, v[:, h * GLA_DV:(h + 1) * GLA_DV], preferred_element_type=F32)
                o_h = o_h + lax.dot_general(qm, st_b, (((1,), (1,)), ((), ())), preferred_element_type=F32)
                outs.append(o_h)
            o_chunk = jnp.concatenate(outs, axis=-1)
            kv = lax.dot_general(v, krem, (((0,), (0,)), ((), ())), preferred_element_type=F32)
            st = st * jnp.exp(b_last)
            for h in range(GLA_HEADS):
                st = st + jnp.where(lane_head == h, kv[h * GLA_DV:(h + 1) * GLA_DV, :], 0.0)
            yield ch, o_chunk
        st_scr[...] = st

    @pl.when(p == 0)
    def _():
        for ch, o_chunk in sweep(0):
            start = pl.multiple_of(j * GLA_TC + ch * c, c)
            ofwd_scr[pl.ds(start, c), :] = o_chunk

    @pl.when(p == 1)
    def _():
        t = nt - 1 - j
        gn = gn_ref[...]
        for ch, o_chunk in sweep(1):
            start = pl.multiple_of(t * GLA_TC + ch * c, c)
            o = ofwd_scr[pl.ds(start, c), :] + o_chunk
            normed = []
            for h in range(GLA_HEADS):
                oh = o[:, h * GLA_DV:(h + 1) * GLA_DV]
                ms = jnp.mean(oh * oh, axis=-1, keepdims=True)
                normed.append(oh * lax.rsqrt(ms + EPS))
            on = jnp.concatenate(normed, axis=-1) * gn
            g = gg_ref[ch * c:(ch + 1) * c, :].astype(F32)
            o_ref[ch * c:(ch + 1) * c, :] = (g * _sigmoid(g) * on).astype(BF16)


def _gla(proj, lr, gla_w2, gla_b, gla_norm, bsz, L):
    T = proj.shape[0]
    nt = L // GLA_TC
    w2p = jnp.zeros((2, LANES, GLA_KEY_WIDTH), F32)
    w2p = w2p.at[0, 0:GLA_RANK].set(gla_w2[0]).at[1, GLA_RANK:2 * GLA_RANK].set(gla_w2[1]).astype(BF16)
    gb = gla_b.reshape(2, 1, GLA_KEY_WIDTH).astype(F32)
    gn = gla_norm.reshape(1, GLA_VAL_WIDTH).astype(F32)

    def tile(b, p, j):
        return b * nt + jnp.where(p == 0, j, nt - 1 - j)

    def otile(b, p, j):
        return b * nt + jnp.where(p == 0, nt - 1, nt - 1 - j)

    kq, kk = COL_GQ // GLA_KEY_WIDTH, COL_GK // GLA_KEY_WIDTH
    kv, kg = COL_GV // GLA_VAL_WIDTH, COL_GG // GLA_VAL_WIDTH
    vmem = (2 * (2 * GLA_TC * GLA_KEY_WIDTH * 2 + 3 * GLA_TC * GLA_VAL_WIDTH * 2 + GLA_TC * LANES * 4)
            + L * GLA_VAL_WIDTH * 4 + GLA_DV * GLA_KEY_WIDTH * 4 + 4 * 1024 * 1024)
    return pl.pallas_call(
        functools.partial(_gla_kernel, nt=nt),
        grid=(bsz, 2, nt),
        in_specs=[
            pl.BlockSpec((GLA_TC, GLA_KEY_WIDTH), lambda b, p, j: (tile(b, p, j), kq)),
            pl.BlockSpec((GLA_TC, GLA_KEY_WIDTH), lambda b, p, j: (tile(b, p, j), kk)),
            pl.BlockSpec((GLA_TC, GLA_VAL_WIDTH), lambda b, p, j: (tile(b, p, j), kv)),
            pl.BlockSpec((GLA_TC, GLA_VAL_WIDTH), lambda b, p, j: (tile(b, p, j), kg)),
            pl.BlockSpec((GLA_TC, LANES), lambda b, p, j: (tile(b, p, j), 0)),
            pl.BlockSpec((2, LANES, GLA_KEY_WIDTH), lambda b, p, j: (0, 0, 0)),
            pl.BlockSpec((2, 1, GLA_KEY_WIDTH), lambda b, p, j: (0, 0, 0)),
            pl.BlockSpec((1, GLA_VAL_WIDTH), lambda b, p, j: (0, 0)),
        ],
        out_specs=pl.BlockSpec((GLA_TC, GLA_VAL_WIDTH), lambda b, p, j: (otile(b, p, j), 0)),
        out_shape=jax.ShapeDtypeStruct((T, GLA_VAL_WIDTH), BF16),
        scratch_shapes=[pltpu.VMEM((L, GLA_VAL_WIDTH), F32), pltpu.VMEM((GLA_DV, GLA_KEY_WIDTH), F32)],
        compiler_params=pltpu.CompilerParams(
            dimension_semantics=("arbitrary", "arbitrary", "arbitrary"), vmem_limit_bytes=_vmem_limit(vmem)),
        name="gla",
    )(proj, proj, proj, proj, lr, w2p, gb, gn)


def _merge_kernel(x_ref, ga_ref, gb_ref, gc_ref, ab_ref, ac_ref, ax_ref, acp_ref, axp_ref, acn_ref, axn_ref,
                  yb_ref, yc_ref, cw_ref, wb_ref, wo_ref, nf_ref, wr_ref, xo_ref, h_ref, aff_ref, *, tiles_per_seq):
    i = pl.program_id(0)
    tm = x_ref.shape[0]
    pos = i % tiles_per_seq
    has_prev = jnp.where(pos == 0, 0.0, 1.0)
    has_next = jnp.where(pos == tiles_per_seq - 1, 0.0, 1.0)
    u = ac_ref[...].astype(F32) * ax_ref[...].astype(F32)
    up = (acp_ref[...].astype(F32) * axp_ref[...].astype(F32))[BF16_SUBLANES - 1:BF16_SUBLANES, :] * has_prev
    un = (acn_ref[...].astype(F32) * axn_ref[...].astype(F32))[0:1, :] * has_next
    ridx = lax.broadcasted_iota(jnp.int32, (tm, 1), 0)
    u_prev = jnp.where(ridx == 0, up, pltpu.roll(u, 1, axis=0))
    u_next = jnp.where(ridx == tm - 1, un, pltpu.roll(u, tm - 1, axis=0))
    cw = cw_ref[...]
    y_a = ab_ref[...].astype(F32) * (u_prev * cw[0:1, :] + u * cw[1:2, :] + u_next * cw[2:3, :])
    m = ga_ref[...].astype(F32) * jnp.dot(y_a.astype(BF16), wb_ref[0], preferred_element_type=F32)
    m = m + gb_ref[...].astype(F32) * jnp.dot(yb_ref[...], wb_ref[1], preferred_element_type=F32)
    m = m + gc_ref[...].astype(F32) * jnp.dot(yc_ref[...], wb_ref[2], preferred_element_type=F32)
    x_new = x_ref[...] + jnp.dot(m.astype(BF16), wo_ref[...], preferred_element_type=F32)
    xo_ref[...] = x_new
    ms = jnp.mean(x_new * x_new, axis=-1, keepdims=True)
    h = x_new * lax.rsqrt(ms + EPS) * nf_ref[...]
    bits = lax.bitcast_convert_type(h.astype(BF16).astype(F32), jnp.uint32)
    half = D_MODEL // 2
    h_ref[...] = (bits[:, :half] >> 16) | (bits[:, half:] & jnp.uint32(0xFFFF0000))
    h0, h1, h2 = _split3(h)
    w0, w1, w2 = wr_ref[0], wr_ref[1], wr_ref[2]
    logits = (jnp.dot(h0, w0, preferred_element_type=F32) + jnp.dot(h0, w1, preferred_element_type=F32)
              + jnp.dot(h1, w0, preferred_element_type=F32) + jnp.dot(h1, w1, preferred_element_type=F32)
              + jnp.dot(h0, w2, preferred_element_type=F32) + jnp.dot(h2, w0, preferred_element_type=F32))
    lane = lax.broadcasted_iota(jnp.int32, logits.shape, 1)
    logits = jnp.where(lane < N_EXPERTS, logits, NEG_INF)
    mx = jnp.max(logits, axis=-1, keepdims=True)
    e = jnp.exp(logits - mx)
    aff_ref[...] = e / jnp.sum(e, axis=-1, keepdims=True)


def _merge(x2d, proj, y_b, y_c, conv_w, wb, wo, norm_ffn, w_router, L):
    T = x2d.shape[0]
    tm = TM_MERGE
    tiles_per_seq = L // tm
    n16 = T // BF16_SUBLANES
    per16 = tm // BF16_SUBLANES
    wr = jnp.zeros((D_MODEL, LANES), F32).at[:, :N_EXPERTS].set(w_router)
    wr = jnp.stack(_split3(wr))
    cb = CONV_WIDTH
    vmem = 2 * (2 * tm * D_MODEL * 4 + 3 * tm * D_MODEL * 2 + 5 * tm * cb * 2 + 4 * BF16_SUBLANES * cb * 2
                + 3 * cb * D_MODEL * 2 + D_MODEL * D_MODEL * 2 + 3 * D_MODEL * LANES * 2
                + tm * D_MODEL * 2 + tm * LANES * 4) + 8 * tm * D_MODEL * 4
    tok = lambda width, col: pl.BlockSpec((tm, width), lambda i: (i, col // width))
    halo_p = lambda col: pl.BlockSpec((BF16_SUBLANES, cb), lambda i: (jnp.maximum(i * per16 - 1, 0), col // cb))
    halo_n = lambda col: pl.BlockSpec((BF16_SUBLANES, cb),
                                      lambda i: (jnp.minimum((i + 1) * per16, n16 - 1), col // cb))
    full = lambda shape: pl.BlockSpec(shape, lambda i: (0,) * len(shape))
    return pl.pallas_call(
        functools.partial(_merge_kernel, tiles_per_seq=tiles_per_seq),
        grid=(T // tm,),
        in_specs=[
            tok(D_MODEL, 0),
            tok(D_MODEL, COL_GATES), tok(D_MODEL, COL_GATES + D_MODEL), tok(D_MODEL, COL_GATES + 2 * D_MODEL),
            tok(cb, COL_AB), tok(cb, COL_AC), tok(cb, COL_AX),
            halo_p(COL_AC), halo_p(COL_AX), halo_n(COL_AC), halo_n(COL_AX),
            tok(NA_WIDTH, 0), tok(GLA_VAL_WIDTH, 0),
            full((CONV_K, cb)), full((N_BRANCH, cb, D_MODEL)), full((D_MODEL, D_MODEL)),
            full((1, D_MODEL)), full((3, D_MODEL, LANES)),
        ],
        out_specs=[tok(D_MODEL, 0), tok(D_MODEL // 2, 0), tok(LANES, 0)],
        out_shape=[
            jax.ShapeDtypeStruct((T, D_MODEL), F32),
            jax.ShapeDtypeStruct((T, D_MODEL // 2), jnp.uint32),
            jax.ShapeDtypeStruct((T, LANES), F32),
        ],
        compiler_params=pltpu.CompilerParams(
            dimension_semantics=("arbitrary",), vmem_limit_bytes=_vmem_limit(vmem)),
        name="merge",
    )(x2d, proj, proj, proj, proj, proj, proj, proj, proj, proj, proj, y_b, y_c,
      conv_w, wb, wo, norm_ffn, wr)


DMA_UNROLL = 8


def _moe_kernel(idx_ref, idxn_ref, g_ref, wg_ref, wu_ref, wd_ref, hp_hbm, o_ref, hbuf, sem_h, *, n_steps):
    tm = hbuf.shape[1]
    s = pl.program_id(0)
    slot = s % 2

    def gather_rows(idx, sl):
        def body(g, c):
            for k in range(DMA_UNROLL):
                r = g * DMA_UNROLL + k
                t = idx[0, 0, r]
                pltpu.make_async_copy(hp_hbm.at[pl.ds(t, 1)], hbuf.at[sl, pl.ds(r, 1)],
                                      sem_h.at[sl]).start(priority=k % 2)
            return c
        lax.fori_loop(0, tm // DMA_UNROLL, body, 0)

    @pl.when(s == 0)
    def _():
        gather_rows(idx_ref, slot)

    @pl.when(s + 1 < n_steps)
    def _():
        gather_rows(idxn_ref, 1 - slot)

    pltpu.make_async_copy(hp_hbm.at[pl.ds(0, tm)], hbuf.at[slot], sem_h.at[slot]).wait()
    w = hbuf[slot]
    lo = lax.bitcast_convert_type(w << 16, F32)
    hi = lax.bitcast_convert_type(w & jnp.uint32(0xFFFF0000), F32)
    xe = jnp.concatenate([lo, hi], axis=1).astype(BF16)
    half = EXPERT_FF // 2
    acc = None
    for f in range(2):
        fs = slice(f * half, (f + 1) * half)
        a = jnp.dot(xe, wg_ref[0, :, fs], preferred_element_type=F32)
        b = jnp.dot(xe, wu_ref[0, :, fs], preferred_element_type=F32)
        he = (a * _sigmoid(a) * b).astype(BF16)
        part = jnp.dot(he, wd_ref[0, fs, :], preferred_element_type=F32)
        acc = part if acc is None else acc + part
    o_ref[...] = (acc * g_ref[0]).astype(BF16)


def _moe_ffn(hp, idx, gate, wg, wu, wd):
    E, cap = idx.shape
    tm = min(TM_FFN, cap)
    assert tm % DMA_UNROLL == 0
    nt = cap // tm
    n_steps = E * nt
    half = D_MODEL // 2
    idx3 = idx.reshape(n_steps, 1, tm).astype(jnp.int32)
    g3 = gate.reshape(n_steps, tm, 1).astype(F32)
    vmem = (2 * (3 * D_MODEL * EXPERT_FF * 2 + tm * LANES * 4 + tm * D_MODEL * 2) + 2 * tm * half * 4
            + 3 * tm * EXPERT_FF * 4)
    smem = lambda f: pl.BlockSpec((1, 1, tm), f, memory_space=pltpu.SMEM)
    wspec = lambda shape: pl.BlockSpec((1,) + shape, lambda s: (s // nt, 0, 0))
    return pl.pallas_call(
        functools.partial(_moe_kernel, n_steps=n_steps),
        grid=(n_steps,),
        in_specs=[
            smem(lambda s: (s, 0, 0)),
            smem(lambda s: (jnp.minimum(s + 1, n_steps - 1), 0, 0)),
            pl.BlockSpec((1, tm, 1), lambda s: (s, 0, 0)),
            wspec((D_MODEL, EXPERT_FF)), wspec((D_MODEL, EXPERT_FF)), wspec((EXPERT_FF, D_MODEL)),
            pl.BlockSpec(memory_space=pl.ANY),
        ],
        out_specs=pl.BlockSpec((tm, D_MODEL), lambda s: (s, 0)),
        out_shape=jax.ShapeDtypeStruct((E * cap, D_MODEL), BF16),
        scratch_shapes=[pltpu.VMEM((2, tm, half), jnp.uint32), pltpu.SemaphoreType.DMA((2,))],
        compiler_params=pltpu.CompilerParams(
            dimension_semantics=("arbitrary",), vmem_limit_bytes=_vmem_limit(vmem)),
        name="moe_ffn",
    )(idx3, idx3, g3, wg, wu, wd, hp)


ROUTE_SLOT_CHUNK = 1024


def _select_kernel(a_ref, sel_ref, *, cap):
    keys = lax.bitcast_convert_type(a_ref[...], jnp.int32)
    n_exp, n_tok = keys.shape
    capf = jnp.float32(cap)

    def count(pred):
        return jnp.sum(pred.astype(F32), axis=1, keepdims=True)

    def value_bit(b, lo):
        cand = lo | jnp.left_shift(jnp.int32(1), 30 - b)
        return jnp.where(count(keys >= cand) >= capf, cand, lo)

    thr = lax.fori_loop(0, 31, value_bit, jnp.zeros((n_exp, 1), jnp.int32))
    gt = keys > thr
    eq = keys == thr
    need = capf - count(gt)
    tok = lax.broadcasted_iota(jnp.int32, keys.shape, 1)
    nbits = int(n_tok).bit_length()

    def index_bit(b, m):
        cand = m | jnp.left_shift(jnp.int32(1), nbits - 1 - b)
        ok = (cand <= n_tok) & (count(eq & (tok < cand)) <= need)
        return jnp.where(ok, cand, m)

    m = lax.fori_loop(0, nbits, index_bit, jnp.zeros((n_exp, 1), jnp.int32))
    sel_ref[...] = (gt | (eq & (tok < m))).astype(F32)


def _compact_kernel(sel_ref, aff_ref, idx_ref, gate_ref, pos_ref, bex_ref, *, cap, n_tok):
    sel = sel_ref[0]
    nb = sel.shape[0]
    ch = min(ROUTE_SLOT_CHUNK, cap)
    nt_dims = (((1,), (1,)), ((), ()))
    r128 = lax.broadcasted_iota(jnp.int32, (LANES, LANES), 0)
    c128 = lax.broadcasted_iota(jnp.int32, (LANES, LANES), 1)
    rj = lax.broadcasted_iota(jnp.int32, (nb, nb), 0)
    cj = lax.broadcasted_iota(jnp.int32, (nb, nb), 1)
    selb = sel.astype(BF16)
    cin = jnp.dot(selb, (r128 <= c128).astype(BF16), preferred_element_type=F32)
    totb = jnp.broadcast_to(cin[:, LANES - 1:LANES], (nb, LANES)).astype(BF16)
    bex = jnp.dot((cj < rj).astype(BF16), totb, preferred_element_type=F32)
    tot_row = lax.dot_general(jnp.ones((8, LANES), BF16), selb, nt_dims, preferred_element_type=F32)
    binc_row = jnp.dot(tot_row.astype(BF16), (rj <= cj).astype(BF16), preferred_element_type=F32)[0:1]
    pos_ref[0] = jnp.where(sel > 0.5, bex + cin - 1.0, -1.0)
    bex_ref[0] = bex[:, 0:1]
    bex_hi = jnp.floor(bex * (1.0 / 256.0))
    bex_lo = bex - 256.0 * bex_hi
    a_hi, a_mid, a_lo = _split3(aff_ref[0])
    table = jnp.concatenate([cin.astype(BF16), bex_hi.astype(BF16), bex_lo.astype(BF16), a_hi, a_mid, a_lo], axis=1)
    lane_nb = lax.broadcasted_iota(jnp.int32, (ch, nb), 1).astype(F32)
    lane128 = lax.broadcasted_iota(jnp.int32, (ch, LANES), 1).astype(F32)
    for c in range(cap // ch):
        s = (lax.broadcasted_iota(jnp.int32, (ch, 1), 0) + c * ch).astype(F32)
        blk = jnp.sum((binc_row <= s).astype(F32), axis=1, keepdims=True)
        onehot = (lane_nb == blk).astype(BF16)
        g = jnp.dot(onehot, table, preferred_element_type=F32)
        cinrow = g[:, 0:LANES]
        before = g[:, LANES:2 * LANES] * 256.0 + g[:, 2 * LANES:3 * LANES]
        affrow = g[:, 3 * LANES:4 * LANES] + g[:, 4 * LANES:5 * LANES] + g[:, 5 * LANES:6 * LANES]
        off = jnp.sum((cinrow <= s - before).astype(F32), axis=1, keepdims=True)
        idx = jnp.minimum(blk * float(LANES) + off, float(n_tok - 1))
        gate = jnp.sum(jnp.where(lane128 == off, affrow, 0.0), axis=1, keepdims=True)
        idx_t = jnp.transpose(jnp.broadcast_to(idx, (ch, LANES)))[0:1]
        gate_t = jnp.transpose(jnp.broadcast_to(gate, (ch, LANES)))[0:1]
        idx_ref[0, :, c * ch:(c + 1) * ch] = idx_t.astype(jnp.int32)
        gate_ref[0, :, c * ch:(c + 1) * ch] = gate_t


def _route(aff, cap):
    n_tok = aff.shape[0]
    nb = n_tok // LANES
    a_t = aff[:, :N_EXPERTS].T
    sel = pl.pallas_call(
        functools.partial(_select_kernel, cap=cap),
        out_shape=jax.ShapeDtypeStruct((N_EXPERTS, n_tok), F32),
        compiler_params=pltpu.CompilerParams(vmem_limit_bytes=_vmem_limit(12 * N_EXPERTS * n_tok * 4)),
        name="route_select",
    )(a_t)
    ch = min(ROUTE_SLOT_CHUNK, cap)
    vmem = 2 * (2 * nb * LANES * 4 + 2 * cap * 4) + 2 * nb * nb * 4 + nb * 6 * LANES * 4 + ch * (nb + 12 * LANES) * 4
    blk = pl.BlockSpec((1, nb, LANES), lambda e: (e, 0, 0))
    oblk = pl.BlockSpec((1, 1, cap), lambda e: (e, 0, 0))
    idx, gate, pos, bex = pl.pallas_call(
        functools.partial(_compact_kernel, cap=cap, n_tok=n_tok),
        grid=(N_EXPERTS,),
        in_specs=[blk, blk],
        out_specs=[oblk, oblk, blk, pl.BlockSpec((1, nb, 1), lambda e: (e, 0, 0))],
        out_shape=[jax.ShapeDtypeStruct((N_EXPERTS, 1, cap), jnp.int32),
                   jax.ShapeDtypeStruct((N_EXPERTS, 1, cap), F32),
                   jax.ShapeDtypeStruct((N_EXPERTS, nb, LANES), F32),
                   jax.ShapeDtypeStruct((N_EXPERTS, nb, 1), F32)],
        compiler_params=pltpu.CompilerParams(
            dimension_semantics=("arbitrary",), vmem_limit_bytes=_vmem_limit(vmem)),
        name="route_compact",
    )(sel.reshape(N_EXPERTS, nb, LANES), a_t.reshape(N_EXPERTS, nb, LANES))
    return (idx.reshape(N_EXPERTS, cap), gate.reshape(N_EXPERTS, cap), pos.reshape(N_EXPERTS, n_tok),
            bex.reshape(N_EXPERTS, nb))


COMBINE_WINDOW = 128
COMBINE_STRIDE = COMBINE_WINDOW - BF16_SUBLANES


def _ple_kernel(lo_ref, cnt_ref, x_ref, pos_ref, p_ref, np_ref, wg_ref, wp_ref, fn_ref, out_hbm, o_ref,
                wbuf, xbuf, sem, sem_x, *, final, cap, n_tiles):
    i = pl.program_id(0)
    slot = i % 2
    tm = x_ref.shape[0]
    W = COMBINE_WINDOW
    n_rows = N_EXPERTS * cap

    def win_start(row):
        aligned = jnp.left_shift(jnp.right_shift(row, 4), 4)
        return pl.multiple_of(jnp.minimum(aligned, n_rows - W), BF16_SUBLANES)

    def issue(tile, sl):
        for e in range(N_EXPERTS):
            st = win_start(e * cap + lo_ref[tile * N_EXPERTS + e])
            pltpu.make_async_copy(out_hbm.at[pl.ds(st, W)], wbuf.at[sl, pl.ds(e * W, W)], sem.at[sl]).start()

    @pl.when(i == 0)
    def _():
        issue(0, 0)

    @pl.when(i + 1 < n_tiles)
    def _():
        issue(i + 1, 1 - slot)

    pltpu.make_async_copy(out_hbm.at[pl.ds(0, N_EXPERTS * W)], wbuf.at[slot], sem.at[slot]).wait()

    posv = pos_ref[...]
    lane = lax.broadcasted_iota(jnp.int32, (tm, W), 1).astype(F32)
    parts = []
    for e in range(N_EXPERTS):
        st = win_start(e * cap + lo_ref[i * N_EXPERTS + e])
        rel = posv[:, e:e + 1] - (st - e * cap).astype(F32)
        parts.append((rel == lane).astype(BF16))
    y = jnp.dot(jnp.concatenate(parts, axis=1), wbuf[slot], preferred_element_type=F32)

    for e in range(N_EXPERTS):
        first = lo_ref[i * N_EXPERTS + e]
        covered_end = win_start(e * cap + first) - e * cap + W
        remaining = jnp.maximum(first + cnt_ref[i * N_EXPERTS + e] - covered_end, 0)
        n_extra = (remaining + COMBINE_STRIDE - 1) // COMBINE_STRIDE
        pe = posv[:, e:e + 1]

        def extra(k, acc, e=e, covered_end=covered_end, pe=pe):
            want = covered_end + k * COMBINE_STRIDE
            st = win_start(e * cap + want)
            cp = pltpu.make_async_copy(out_hbm.at[pl.ds(st, W)], xbuf, sem_x)
            cp.start()
            cp.wait()
            hit = (pe - (st - e * cap).astype(F32) == lane) & (pe >= want.astype(F32)) \
                & (pe < (want + COMBINE_STRIDE).astype(F32))
            return acc + jnp.dot(hit.astype(BF16), xbuf[...], preferred_element_type=F32)

        y = lax.fori_loop(0, n_extra, extra, y)

    x = x_ref[...] + y
    ms = jnp.mean(x * x, axis=-1, keepdims=True)
    h = (x * lax.rsqrt(ms + EPS) * np_ref[...]).astype(BF16)
    gate = _sigmoid(jnp.dot(h, wg_ref[...], preferred_element_type=F32))
    pp = jnp.dot(p_ref[...].astype(BF16), wp_ref[...], preferred_element_type=F32)
    x = x + gate * pp
    if final:
        ms = jnp.mean(x * x, axis=-1, keepdims=True)
        x = x * lax.rsqrt(ms + EPS) * fn_ref[...]
    o_ref[...] = x


def _ple(x2d, out, pos, bex, p2d, norm_ple, wg, wp, final_norm, final):
    T = x2d.shape[0]
    tm = min(TM_PLE, T)
    n_tiles = T // tm
    cap = out.shape[0] // N_EXPERTS
    W = COMBINE_WINDOW
    assert cap >= W and cap % BF16_SUBLANES == 0 and tm % LANES == 0
    lo = bex[:, ::tm // LANES].astype(jnp.int32)
    cnt = jnp.diff(lo, axis=1, append=jnp.full((N_EXPERTS, 1), cap, jnp.int32))
    lo = lo.T.reshape(-1)
    cnt = cnt.T.reshape(-1)
    pos_t = pos.T
    vmem = (2 * (2 * tm * D_MODEL * 4 + tm * LANES * 4 + tm * PLE_DIM * 4 + D_MODEL * D_MODEL * 2
                 + PLE_DIM * D_MODEL * 2) + 2 * N_EXPERTS * W * D_MODEL * 2 + W * D_MODEL * 2
            + tm * N_EXPERTS * W * 2 + 6 * tm * D_MODEL * 4)
    tok = lambda width: pl.BlockSpec((tm, width), lambda i, lo, cnt: (i, 0))
    full = lambda shape: pl.BlockSpec(shape, lambda i, lo, cnt: (0,) * len(shape))
    grid_spec = pltpu.PrefetchScalarGridSpec(
        num_scalar_prefetch=2,
        grid=(n_tiles,),
        in_specs=[tok(D_MODEL), tok(N_EXPERTS), tok(PLE_DIM), full((1, D_MODEL)), full((D_MODEL, D_MODEL)),
                  full((PLE_DIM, D_MODEL)), full((1, D_MODEL)), pl.BlockSpec(memory_space=pl.ANY)],
        out_specs=tok(D_MODEL),
        scratch_shapes=[
            pltpu.VMEM((2, N_EXPERTS * W, D_MODEL), BF16),
            pltpu.VMEM((W, D_MODEL), BF16),
            pltpu.SemaphoreType.DMA((2,)),
            pltpu.SemaphoreType.DMA,
        ],
    )
    return pl.pallas_call(
        functools.partial(_ple_kernel, final=final, cap=cap, n_tiles=n_tiles),
        grid_spec=grid_spec,
        out_shape=jax.ShapeDtypeStruct((T, D_MODEL), F32),
        compiler_params=pltpu.CompilerParams(
            dimension_semantics=("arbitrary",), vmem_limit_bytes=_vmem_limit(vmem)),
        name="combine_ple",
    )(lo, cnt, x2d, pos_t, p2d, norm_ple, wg, wp, final_norm, out)


def _prep_layer(i, norm_mix, w_in, conv_w, na_rpb, gla_w2, gla_b, gla_norm, w_branch, w_out, norm_ffn, w_router,
                w_exp_gate, w_exp_up, w_exp_down, norm_ple, w_ple_gate, w_ple_proj):
    w = w_in[i]
    sizes = (CONV_WIDTH,) * 3 + (NA_WIDTH,) * 3 + (GLA_KEY_WIDTH,) * 2 + (GLA_VAL_WIDTH,) * 2 + (2 * GLA_RANK,)
    offs = np.concatenate([[0], np.cumsum(sizes)])
    gates_off = int(offs[-1])
    lr_off = int(offs[10])
    w_main = jnp.concatenate([w[:, gates_off:], w[:, :lr_off]], axis=1)
    assert float(np.log2(NA_HEAD_DIM ** -0.5)).is_integer()
    col_scale = np.ones((PROJ_COLS,), np.float32)
    col_scale[COL_NQ:COL_NQ + NA_WIDTH] = NA_HEAD_DIM ** -0.5
    w_main = (w_main * col_scale).astype(BF16)
    w_lr = jnp.zeros((D_MODEL, LANES), F32).at[:, :2 * GLA_RANK].set(w[:, lr_off:gates_off]).astype(BF16)
    return dict(
        norm_mix=norm_mix[i].reshape(1, D_MODEL), w_main=w_main, w_lr=w_lr, conv_w=conv_w[i], na_rpb=na_rpb[i],
        gla_w2=gla_w2[i], gla_b=gla_b[i], gla_norm=gla_norm[i], w_branch=w_branch[i].astype(BF16),
        w_out=w_out[i].astype(BF16), norm_ffn=norm_ffn[i].reshape(1, D_MODEL), w_router=w_router[i],
        w_exp_gate=w_exp_gate[i].astype(BF16), w_exp_up=w_exp_up[i].astype(BF16),
        w_exp_down=w_exp_down[i].astype(BF16), norm_ple=norm_ple[i].reshape(1, D_MODEL),
        w_ple_gate=w_ple_gate[i].astype(BF16), w_ple_proj=w_ple_proj[i].astype(BF16))


def _moe(hp, aff, lp):
    n_tok = hp.shape[0]
    cap = CAPACITY_FACTOR * n_tok // N_EXPERTS
    idx, gate_val, pos, bex = _route(aff, cap)
    out = _moe_ffn(hp, idx, gate_val, lp["w_exp_gate"], lp["w_exp_up"], lp["w_exp_down"])
    return out, pos, bex


def _trunk(x, p, layers, final_norm):
    bsz, L, _ = x.shape
    T = bsz * L
    x2d = x.reshape(T, D_MODEL)
    fn = final_norm.reshape(1, D_MODEL)
    for i, lp in enumerate(layers):
        proj, lr = _inproj(x2d, lp["norm_mix"], lp["w_main"], lp["w_lr"])
        y_b = _na_attention(proj, lp["na_rpb"], bsz, L)
        y_c = _gla(proj, lr, lp["gla_w2"], lp["gla_b"], lp["gla_norm"], bsz, L)
        x2d, h, aff = _merge(x2d, proj, y_b, y_c, lp["conv_w"], lp["w_branch"], lp["w_out"], lp["norm_ffn"],
                             lp["w_router"], L)
        out, pos, bex = _moe(h, aff, lp)
        x2d = _ple(x2d, out, pos, bex, p[i].reshape(T, PLE_DIM), lp["norm_ple"], lp["w_ple_gate"], lp["w_ple_proj"], fn,
                   final=(i == len(layers) - 1))
    return x2d.reshape(bsz, L, D_MODEL)


def kernel(x_prompt, x_sample, p_prompt, p_sample, norm_mix, w_in, conv_w, na_rpb, gla_w2, gla_b, gla_norm, w_branch, w_out, norm_ffn, w_router, w_exp_gate, w_exp_up, w_exp_down, norm_ple, w_ple_gate, w_ple_proj, final_norm):
    depth = w_in.shape[0]
    layers = [_prep_layer(i, norm_mix, w_in, conv_w, na_rpb, gla_w2, gla_b, gla_norm, w_branch, w_out, norm_ffn,
                          w_router, w_exp_gate, w_exp_up, w_exp_down, norm_ple, w_ple_gate, w_ple_proj)
              for i in range(depth)]
    y_prompt = _trunk(x_prompt, p_prompt, layers, final_norm)
    y_sample = _trunk(x_sample, p_sample, layers, final_norm)
    return (y_prompt, y_sample)
```

```python
import functools

import numpy as np
import jax
import jax.numpy as jnp
from jax import lax
from jax.experimental import pallas as pl
from jax.experimental.pallas import tpu as pltpu

F32 = jnp.float32
BF16 = jnp.bfloat16

D_MODEL = 1024
GRID_W = 64
PLE_DIM = 256
EPS = 1e-6
CONV_WIDTH = 512
CONV_K = 3
NA_HEADS = 8
NA_HEAD_DIM = 64
NA_WIDTH = NA_HEADS * NA_HEAD_DIM
WIN_ROWS = 8
WIN_COLS = 16
GLA_HEADS = 4
GLA_DK = 64
GLA_DV = 128
GLA_KEY_WIDTH = GLA_HEADS * GLA_DK
GLA_VAL_WIDTH = GLA_HEADS * GLA_DV
GLA_RANK = 16
GLA_GATE_NORM = 16.0
GLA_CHUNK = 64
N_BRANCH = 3
N_EXPERTS = 16
EXPERT_FF = 2048
CAPACITY_FACTOR = 2

V7X_VMEM_BYTES = 64 * 1024 * 1024
LANES = 128
BF16_SUBLANES = 16

PROJ_TN = 512
COL_GATES = 0
COL_AB = 3 * D_MODEL
COL_AC = COL_AB + CONV_WIDTH
COL_AX = COL_AC + CONV_WIDTH
COL_NQ = COL_AX + CONV_WIDTH
COL_NK = COL_NQ + NA_WIDTH
COL_NV = COL_NK + NA_WIDTH
COL_GQ = COL_NV + NA_WIDTH
COL_GK = COL_GQ + GLA_KEY_WIDTH
COL_GV = COL_GK + GLA_KEY_WIDTH
COL_GG = COL_GV + GLA_VAL_WIDTH
PROJ_COLS = COL_GG + GLA_VAL_WIDTH
N_GATE_BLOCKS = (3 * D_MODEL) // PROJ_TN

TM_INPROJ = 512
NA_TILE_ROWS = 4
NA_TQ = NA_TILE_ROWS * GRID_W
GLA_TC = 512
TM_MERGE = 512
TM_FFN = 512
TM_PLE = 512

NEG_INF = -1e30


def _vmem_limit(nbytes):
    return int(min(V7X_VMEM_BYTES - 6 * 1024 * 1024, max(32 * 1024 * 1024, nbytes + 8 * 1024 * 1024)))


def _sigmoid(x):
    return 1.0 / (1.0 + jnp.exp(-x))


def _split3(a):
    hi = a.astype(BF16)
    r1 = a - hi.astype(F32)
    mid = r1.astype(BF16)
    lo = (r1 - mid.astype(F32)).astype(BF16)
    return hi, mid, lo


def _inproj_kernel(x_ref, nw_ref, wlr_ref, w_hbm, proj_ref, lr_ref, w_vmem, sem):
    @pl.when(pl.program_id(0) == 0)
    def _():
        cp = pltpu.make_async_copy(w_hbm, w_vmem, sem)
        cp.start()
        cp.wait()

    x = x_ref[...]
    ms = jnp.mean(x * x, axis=-1, keepdims=True)
    hb = (x * lax.rsqrt(ms + EPS) * nw_ref[...]).astype(BF16)
    lr_ref[...] = jnp.dot(hb, wlr_ref[...], preferred_element_type=F32)
    for j in range(PROJ_COLS // PROJ_TN):
        cs = slice(j * PROJ_TN, (j + 1) * PROJ_TN)
        acc = jnp.dot(hb, w_vmem[:, cs], preferred_element_type=F32)
        if j < N_GATE_BLOCKS:
            acc = _sigmoid(acc)
        proj_ref[:, cs] = acc.astype(BF16)


def _inproj(x2d, norm_w, w_main, w_lr):
    T = x2d.shape[0]
    tm = min(TM_INPROJ, T)
    vmem = (2 * (tm * D_MODEL * 4 + D_MODEL * LANES * 2 + tm * PROJ_COLS * 2 + tm * LANES * 4)
            + D_MODEL * PROJ_COLS * 2 + tm * D_MODEL * 2 + 4 * tm * PROJ_TN * 4)
    return pl.pallas_call(
        _inproj_kernel,
        grid=(T // tm,),
        in_specs=[
            pl.BlockSpec((tm, D_MODEL), lambda i: (i, 0)),
            pl.BlockSpec((1, D_MODEL), lambda i: (0, 0)),
            pl.BlockSpec((D_MODEL, LANES), lambda i: (0, 0)),
            pl.BlockSpec(memory_space=pl.ANY),
        ],
        out_specs=[
            pl.BlockSpec((tm, PROJ_COLS), lambda i: (i, 0)),
            pl.BlockSpec((tm, LANES), lambda i: (i, 0)),
        ],
        out_shape=[
            jax.ShapeDtypeStruct((T, PROJ_COLS), BF16),
            jax.ShapeDtypeStruct((T, LANES), F32),
        ],
        scratch_shapes=[pltpu.VMEM((D_MODEL, PROJ_COLS), BF16), pltpu.SemaphoreType.DMA],
        compiler_params=pltpu.CompilerParams(
            dimension_semantics=("arbitrary",), vmem_limit_bytes=_vmem_limit(vmem)),
        name="inproj",
    )(x2d, norm_w, w_lr, w_main)


def _na_tables(rows):
    R = NA_TILE_ROWS
    J = rows // R
    kr = min(WIN_ROWS, rows)
    qr_rel = np.arange(NA_TQ) // GRID_W
    qc = np.arange(NA_TQ) % GRID_W
    kk = np.arange(3 * NA_TQ)
    kblk = kk // NA_TQ
    kwithin = kk % NA_TQ
    kc = kwithin % GRID_W
    col_start = np.clip(qc - WIN_COLS // 2, 0, GRID_W - WIN_COLS)
    col_ok = (kc[None, :] >= col_start[:, None]) & (kc[None, :] < col_start[:, None] + WIN_COLS)
    dc = np.clip(kc[None, :] - qc[:, None] + WIN_COLS - 1, 0, 2 * WIN_COLS - 2)
    types, keys, tile_type = [], {}, []
    for j in range(J):
        blocks = np.array([max(j - 1, 0), j, min(j + 1, J - 1)])
        blk_ok = np.array([j - 1 >= 0, True, j + 1 <= J - 1])
        krow = blocks[kblk] * R + kwithin // GRID_W
        qrow = j * R + qr_rel
        rs = np.clip(qrow - kr // 2, 0, rows - kr)
        row_ok = (krow[None, :] >= rs[:, None]) & (krow[None, :] < rs[:, None] + kr) & blk_ok[kblk][None, :]
        mask = row_ok & col_ok
        dr = np.clip(krow[None, :] - qrow[:, None] + WIN_ROWS - 1, 0, 2 * WIN_ROWS - 2)
        dr = np.where(mask, dr, 0)
        key = mask.tobytes() + dr.astype(np.int8).tobytes()
        if key not in keys:
            keys[key] = len(types)
            types.append((mask, dr, np.where(mask, dc, 0)))
        tile_type.append(keys[key])
    mask = np.stack([t[0] for t in types])
    dr = np.stack([t[1] for t in types]).astype(np.int32)
    dcs = np.stack([t[2] for t in types]).astype(np.int32)
    return mask, dr, dcs, np.array(tile_type, np.int32)


def _na_bias(rpb, mask, dr):
    n_types = mask.shape[0]
    W = GRID_W
    qc = np.arange(W)[:, None]
    kc = np.arange(W)[None, :]
    col_start = np.clip(qc - WIN_COLS // 2, 0, W - WIN_COLS)
    col_ok = (kc >= col_start) & (kc < col_start + WIN_COLS)
    n_dc = 2 * WIN_COLS - 1
    lead = W - WIN_COLS
    padded = jnp.pad(rpb, ((0, 0), (0, 0), (lead, 2 * W - 1 - lead - n_dc)), constant_values=NEG_INF)
    toep = jnp.stack([padded[:, :, W - 1 - q:2 * W - 1 - q] for q in range(W)], axis=2)
    toep = jnp.where(col_ok[None, None], toep, NEG_INF)
    dead = jnp.full((NA_HEADS, 1, W, W), NEG_INF, F32)
    blocks = jnp.concatenate([toep, dead], axis=1)
    n_dead = toep.shape[1]
    tables = []
    for t in range(n_types):
        ids = np.where(mask[t, ::W, ::W], dr[t, ::W, ::W], n_dead)
        rows_ = [jnp.concatenate([blocks[:, int(b)] for b in ids_row], axis=-1) for ids_row in ids]
        tables.append(jnp.concatenate(rows_, axis=1))
    return jnp.stack(tables)


def _na_kernel(tid_ref, q_ref, kp_ref, kc_ref, kn_ref, vp_ref, vc_ref, vn_ref, bias_ref, o_ref):
    del tid_ref
    q = q_ref[...]
    k = jnp.concatenate([kp_ref[...], kc_ref[...], kn_ref[...]], axis=0)
    v = jnp.concatenate([vp_ref[...], vc_ref[...], vn_ref[...]], axis=0)
    outs = []
    for h in range(NA_HEADS):
        sl = slice(h * NA_HEAD_DIM, (h + 1) * NA_HEAD_DIM)
        s = lax.dot_general(q[:, sl], k[:, sl], (((1,), (1,)), ((), ())), preferred_element_type=F32)
        s = s + bias_ref[0, h]
        m = jnp.max(s, axis=-1, keepdims=True)
        p = jnp.exp(s - m)
        l = jnp.sum(p, axis=-1, keepdims=True)
        o = jnp.dot(p.astype(BF16), v[:, sl], preferred_element_type=F32)
        outs.append(o * (1.0 / l))
    o_ref[...] = jnp.concatenate(outs, axis=-1).astype(BF16)


def _na_attention(proj, rpb, bsz, L):
    T = proj.shape[0]
    rows = L // GRID_W
    J = rows // NA_TILE_ROWS
    mask, dr, _, tile_type = _na_tables(rows)
    bias = _na_bias(rpb.astype(F32), mask, dr)
    qb, kb, vb = COL_NQ // NA_WIDTH, COL_NK // NA_WIDTH, COL_NV // NA_WIDTH

    def prev(b, j, tid):
        return b * J + jnp.maximum(j - 1, 0)

    def nxt(b, j, tid):
        return b * J + jnp.minimum(j + 1, J - 1)

    blk = (NA_TQ, NA_WIDTH)
    vmem = 2 * (8 * NA_TQ * NA_WIDTH * 2 + NA_HEADS * NA_TQ * 3 * NA_TQ * 4) + 8 * NA_TQ * 3 * NA_TQ * 4
    grid_spec = pltpu.PrefetchScalarGridSpec(
        num_scalar_prefetch=1,
        grid=(bsz, J),
        in_specs=[
            pl.BlockSpec(blk, lambda b, j, tid: (b * J + j, qb)),
            pl.BlockSpec(blk, lambda b, j, tid: (prev(b, j, tid), kb)),
            pl.BlockSpec(blk, lambda b, j, tid: (b * J + j, kb)),
            pl.BlockSpec(blk, lambda b, j, tid: (nxt(b, j, tid), kb)),
            pl.BlockSpec(blk, lambda b, j, tid: (prev(b, j, tid), vb)),
            pl.BlockSpec(blk, lambda b, j, tid: (b * J + j, vb)),
            pl.BlockSpec(blk, lambda b, j, tid: (nxt(b, j, tid), vb)),
            pl.BlockSpec((1, NA_HEADS, NA_TQ, 3 * NA_TQ), lambda b, j, tid: (tid[j], 0, 0, 0)),
        ],
        out_specs=pl.BlockSpec(blk, lambda b, j, tid: (b * J + j, 0)),
    )
    return pl.pallas_call(
        _na_kernel,
        grid_spec=grid_spec,
        out_shape=jax.ShapeDtypeStruct((T, NA_WIDTH), BF16),
        compiler_params=pltpu.CompilerParams(
            dimension_semantics=("arbitrary", "arbitrary"), vmem_limit_bytes=_vmem_limit(vmem)),
        name="na_attention",
    )(jnp.asarray(tile_type), proj, proj, proj, proj, proj, proj, proj, bias)


def _gla_kernel(fq_ref, fk_ref, fv_ref, flr_ref, bq_ref, bk_ref, bv_ref, blr_ref, cum_ref, w2_ref, gb_ref,
                of_ref, ob_ref, stf_scr, stb_scr):
    j = pl.program_id(1)
    c = GLA_CHUNK
    n_chunks = GLA_TC // c

    @pl.when(j == 0)
    def _():
        stf_scr[...] = jnp.zeros_like(stf_scr)
        stb_scr[...] = jnp.zeros_like(stb_scr)

    lane_head = lax.broadcasted_iota(jnp.int32, (1, GLA_KEY_WIDTH), 1) // GLA_DK
    ri = lax.broadcasted_iota(jnp.int32, (c, c), 0)
    ci = lax.broadcasted_iota(jnp.int32, (c, c), 1)

    def sweep(direction, q_ref, k_ref, v_ref, lr_ref, st_scr, o_ref):
        tri = (ci <= ri) if direction == 0 else (ci >= ri)
        z = lr_ref[...].astype(BF16)
        xg = jnp.dot(z, w2_ref[direction], preferred_element_type=F32) + gb_ref[direction]
        la = (jnp.minimum(xg, 0.0) - jnp.log1p(jnp.exp(-jnp.abs(xg)))) * (1.0 / GLA_GATE_NORM)
        hi, mid, lo = _split3(la)
        cum = cum_ref[direction]
        b_all = (jnp.dot(cum, hi, preferred_element_type=F32)
                 + jnp.dot(cum, mid, preferred_element_type=F32)
                 + jnp.dot(cum, lo, preferred_element_type=F32))
        k_all = k_ref[...].astype(F32)
        qd_all = q_ref[...].astype(F32) * jnp.exp(b_all)
        kd_all = (k_all * jnp.exp(-b_all)).astype(BF16)
        st = st_scr[...]
        order = range(n_chunks) if direction == 0 else range(n_chunks - 1, -1, -1)
        for ch in order:
            rs = slice(ch * c, (ch + 1) * c)
            b = b_all[rs]
            b_last = b[c - 1:c, :] if direction == 0 else b[0:1, :]
            v = v_ref[rs, :]
            qd = qd_all[rs]
            kd = kd_all[rs]
            krem = (k_all[rs] * jnp.exp(b_last - b)).astype(BF16)
            st_b = st.astype(BF16)
            outs = []
            for h in range(GLA_HEADS):
                qm = jnp.where(lane_head == h, qd, 0.0).astype(BF16)
                att = lax.dot_general(qm, kd, (((1,), (1,)), ((), ())), preferred_element_type=F32)
                att = jnp.where(tri, att, 0.0).astype(BF16)
                o_h = jnp.dot(att, v[:, h * GLA_DV:(h + 1) * GLA_DV], preferred_element_type=F32)
                o_h = o_h + lax.dot_general(qm, st_b, (((1,), (1,)), ((), ())), preferred_element_type=F32)
                outs.append(o_h)
            o_ref[rs, :] = jnp.concatenate(outs, axis=-1).astype(BF16)
            kv = lax.dot_general(v, krem, (((0,), (0,)), ((), ())), preferred_element_type=F32)
            st = st * jnp.exp(b_last)
            for h in range(GLA_HEADS):
                st = st + jnp.where(lane_head == h, kv[h * GLA_DV:(h + 1) * GLA_DV, :], 0.0)
        st_scr[...] = st

    sweep(0, fq_ref, fk_ref, fv_ref, flr_ref, stf_scr, of_ref)
    sweep(1, bq_ref, bk_ref, bv_ref, blr_ref, stb_scr, ob_ref)


def _gla(proj, lr, gla_w2, gla_b, bsz, L):
    T = proj.shape[0]
    nt = L // GLA_TC
    w2p = jnp.zeros((2, LANES, GLA_KEY_WIDTH), F32)
    w2p = w2p.at[0, 0:GLA_RANK].set(gla_w2[0]).at[1, GLA_RANK:2 * GLA_RANK].set(gla_w2[1]).astype(BF16)
    gb = gla_b.reshape(2, 1, GLA_KEY_WIDTH).astype(F32)
    r = np.arange(GLA_TC)
    same = (r[:, None] // GLA_CHUNK) == (r[None, :] // GLA_CHUNK)
    cum = jnp.asarray(np.stack([same & (r[None, :] <= r[:, None]), same & (r[None, :] >= r[:, None])]), BF16)

    fwd = lambda b, j: b * nt + j
    bwd = lambda b, j: b * nt + nt - 1 - j
    kq, kk, kv = COL_GQ // GLA_KEY_WIDTH, COL_GK // GLA_KEY_WIDTH, COL_GV // GLA_VAL_WIDTH
    vmem = (2 * 2 * (2 * GLA_TC * GLA_KEY_WIDTH * 2 + 2 * GLA_TC * GLA_VAL_WIDTH * 2 + GLA_TC * LANES * 4)
            + 2 * 2 * GLA_TC * GLA_TC * 2 + 2 * GLA_DV * GLA_KEY_WIDTH * 4 + 16 * GLA_TC * GLA_KEY_WIDTH * 4)

    def specs(tile):
        return [
            pl.BlockSpec((GLA_TC, GLA_KEY_WIDTH), lambda b, j: (tile(b, j), kq)),
            pl.BlockSpec((GLA_TC, GLA_KEY_WIDTH), lambda b, j: (tile(b, j), kk)),
            pl.BlockSpec((GLA_TC, GLA_VAL_WIDTH), lambda b, j: (tile(b, j), kv)),
            pl.BlockSpec((GLA_TC, LANES), lambda b, j: (tile(b, j), 0)),
        ]

    out_sds = jax.ShapeDtypeStruct((T, GLA_VAL_WIDTH), BF16)
    return pl.pallas_call(
        _gla_kernel,
        grid=(bsz, nt),
        in_specs=specs(fwd) + specs(bwd) + [
            pl.BlockSpec((2, GLA_TC, GLA_TC), lambda b, j: (0, 0, 0)),
            pl.BlockSpec((2, LANES, GLA_KEY_WIDTH), lambda b, j: (0, 0, 0)),
            pl.BlockSpec((2, 1, GLA_KEY_WIDTH), lambda b, j: (0, 0, 0)),
        ],
        out_specs=[pl.BlockSpec((GLA_TC, GLA_VAL_WIDTH), lambda b, j: (fwd(b, j), 0)),
                   pl.BlockSpec((GLA_TC, GLA_VAL_WIDTH), lambda b, j: (bwd(b, j), 0))],
        out_shape=[out_sds, out_sds],
        scratch_shapes=[pltpu.VMEM((GLA_DV, GLA_KEY_WIDTH), F32), pltpu.VMEM((GLA_DV, GLA_KEY_WIDTH), F32)],
        compiler_params=pltpu.CompilerParams(
            dimension_semantics=("arbitrary", "arbitrary"), vmem_limit_bytes=_vmem_limit(vmem)),
        name="gla",
    )(proj, proj, proj, lr, proj, proj, proj, lr, cum, w2p, gb)


def _merge_kernel(x_ref, ga_ref, gb_ref, gc_ref, ab_ref, ac_ref, ax_ref, acp_ref, axp_ref, acn_ref, axn_ref,
                  yb_ref, of_ref, ob_ref, gg_ref, gn_ref, cw_ref, wb_ref, wo_ref, nf_ref, wr_ref,
                  xo_ref, h_ref, aff_ref, *, tiles_per_seq):
    i = pl.program_id(0)
    tm = x_ref.shape[0]
    pos = i % tiles_per_seq
    has_prev = jnp.where(pos == 0, 0.0, 1.0)
    has_next = jnp.where(pos == tiles_per_seq - 1, 0.0, 1.0)
    u = ac_ref[...].astype(F32) * ax_ref[...].astype(F32)
    up = (acp_ref[...].astype(F32) * axp_ref[...].astype(F32))[BF16_SUBLANES - 1:BF16_SUBLANES, :] * has_prev
    un = (acn_ref[...].astype(F32) * axn_ref[...].astype(F32))[0:1, :] * has_next
    ridx = lax.broadcasted_iota(jnp.int32, (tm, 1), 0)
    u_prev = jnp.where(ridx == 0, up, pltpu.roll(u, 1, axis=0))
    u_next = jnp.where(ridx == tm - 1, un, pltpu.roll(u, tm - 1, axis=0))
    cw = cw_ref[...]
    y_a = ab_ref[...].astype(F32) * (u_prev * cw[0:1, :] + u * cw[1:2, :] + u_next * cw[2:3, :])
    m = ga_ref[...].astype(F32) * jnp.dot(y_a.astype(BF16), wb_ref[0], preferred_element_type=F32)
    m = m + gb_ref[...].astype(F32) * jnp.dot(yb_ref[...], wb_ref[1], preferred_element_type=F32)
    o = of_ref[...].astype(F32) + ob_ref[...].astype(F32)
    normed = []
    for hd in range(GLA_HEADS):
        oh = o[:, hd * GLA_DV:(hd + 1) * GLA_DV]
        normed.append(oh * lax.rsqrt(jnp.mean(oh * oh, axis=-1, keepdims=True) + EPS))
    g = gg_ref[...].astype(F32)
    y_c = (g * _sigmoid(g) * (jnp.concatenate(normed, axis=-1) * gn_ref[...])).astype(BF16)
    m = m + gc_ref[...].astype(F32) * jnp.dot(y_c, wb_ref[2], preferred_element_type=F32)
    x_new = x_ref[...] + jnp.dot(m.astype(BF16), wo_ref[...], preferred_element_type=F32)
    xo_ref[...] = x_new
    ms = jnp.mean(x_new * x_new, axis=-1, keepdims=True)
    h = x_new * lax.rsqrt(ms + EPS) * nf_ref[...]
    bits = lax.bitcast_convert_type(h.astype(BF16).astype(F32), jnp.uint32)
    half = D_MODEL // 2
    h_ref[...] = (bits[:, :half] >> 16) | (bits[:, half:] & jnp.uint32(0xFFFF0000))
    h0, h1, h2 = _split3(h)
    w0, w1, w2 = wr_ref[0], wr_ref[1], wr_ref[2]
    logits = (jnp.dot(h0, w0, preferred_element_type=F32) + jnp.dot(h0, w1, preferred_element_type=F32)
              + jnp.dot(h1, w0, preferred_element_type=F32) + jnp.dot(h1, w1, preferred_element_type=F32)
              + jnp.dot(h0, w2, preferred_element_type=F32) + jnp.dot(h2, w0, preferred_element_type=F32))
    lane = lax.broadcasted_iota(jnp.int32, logits.shape, 1)
    logits = jnp.where(lane < N_EXPERTS, logits, NEG_INF)
    mx = jnp.max(logits, axis=-1, keepdims=True)
    e = jnp.exp(logits - mx)
    aff_ref[...] = e / jnp.sum(e, axis=-1, keepdims=True)


def _merge(x2d, proj, y_b, o_f, o_b, gla_norm, conv_w, wb, wo, norm_ffn, w_router, L):
    T = x2d.shape[0]
    tm = min(TM_MERGE, L)
    gn = gla_norm.reshape(1, GLA_VAL_WIDTH).astype(F32)
    tiles_per_seq = L // tm
    n16 = T // BF16_SUBLANES
    per16 = tm // BF16_SUBLANES
    wr = jnp.zeros((D_MODEL, LANES), F32).at[:, :N_EXPERTS].set(w_router)
    wr = jnp.stack(_split3(wr))
    cb = CONV_WIDTH
    vmem = 2 * (2 * tm * D_MODEL * 4 + 3 * tm * D_MODEL * 2 + 5 * tm * cb * 2 + 4 * BF16_SUBLANES * cb * 2
                + 3 * cb * D_MODEL * 2 + D_MODEL * D_MODEL * 2 + 3 * D_MODEL * LANES * 2
                + tm * D_MODEL * 2 + tm * LANES * 4) + 8 * tm * D_MODEL * 4
    tok = lambda width, col: pl.BlockSpec((tm, width), lambda i: (i, col // width))
    halo_p = lambda col: pl.BlockSpec((BF16_SUBLANES, cb), lambda i: (jnp.maximum(i * per16 - 1, 0), col // cb))
    halo_n = lambda col: pl.BlockSpec((BF16_SUBLANES, cb),
                                      lambda i: (jnp.minimum((i + 1) * per16, n16 - 1), col // cb))
    full = lambda shape: pl.BlockSpec(shape, lambda i: (0,) * len(shape))
    return pl.pallas_call(
        functools.partial(_merge_kernel, tiles_per_seq=tiles_per_seq),
        grid=(T // tm,),
        in_specs=[
            tok(D_MODEL, 0),
            tok(D_MODEL, COL_GATES), tok(D_MODEL, COL_GATES + D_MODEL), tok(D_MODEL, COL_GATES + 2 * D_MODEL),
            tok(cb, COL_AB), tok(cb, COL_AC), tok(cb, COL_AX),
            halo_p(COL_AC), halo_p(COL_AX), halo_n(COL_AC), halo_n(COL_AX),
            tok(NA_WIDTH, 0), tok(GLA_VAL_WIDTH, 0), tok(GLA_VAL_WIDTH, 0), tok(GLA_VAL_WIDTH, COL_GG),
            full((1, GLA_VAL_WIDTH)), full((CONV_K, cb)), full((N_BRANCH, cb, D_MODEL)), full((D_MODEL, D_MODEL)),
            full((1, D_MODEL)), full((3, D_MODEL, LANES)),
        ],
        out_specs=[tok(D_MODEL, 0), tok(D_MODEL // 2, 0), tok(LANES, 0)],
        out_shape=[
            jax.ShapeDtypeStruct((T, D_MODEL), F32),
            jax.ShapeDtypeStruct((T, D_MODEL // 2), jnp.uint32),
            jax.ShapeDtypeStruct((T, LANES), F32),
        ],
        compiler_params=pltpu.CompilerParams(
            dimension_semantics=("arbitrary",), vmem_limit_bytes=_vmem_limit(vmem)),
        name="merge",
    )(x2d, proj, proj, proj, proj, proj, proj, proj, proj, proj, proj, y_b, o_f, o_b, proj, gn,
      conv_w, wb, wo, norm_ffn, wr)


DMA_UNROLL = 8


def _moe_kernel(idx_ref, idxn_ref, g_ref, wg_ref, wu_ref, wd_ref, hp_hbm, o_ref, hbuf, sem_h, *, n_steps):
    tm = hbuf.shape[1]
    s = pl.program_id(0)
    slot = s % 2

    def gather_rows(idx, sl):
        def body(g, c):
            for k in range(DMA_UNROLL):
                r = g * DMA_UNROLL + k
                t = idx[0, 0, r]
                pltpu.make_async_copy(hp_hbm.at[pl.ds(t, 1)], hbuf.at[sl, pl.ds(r, 1)],
                                      sem_h.at[sl]).start(priority=k % 2)
            return c
        lax.fori_loop(0, tm // DMA_UNROLL, body, 0)

    @pl.when(s == 0)
    def _():
        gather_rows(idx_ref, slot)

    @pl.when(s + 1 < n_steps)
    def _():
        gather_rows(idxn_ref, 1 - slot)

    pltpu.make_async_copy(hp_hbm.at[pl.ds(0, tm)], hbuf.at[slot], sem_h.at[slot]).wait()
    w = hbuf[slot]
    lo = lax.bitcast_convert_type(w << 16, F32)
    hi = lax.bitcast_convert_type(w & jnp.uint32(0xFFFF0000), F32)
    xe = jnp.concatenate([lo, hi], axis=1).astype(BF16)
    half = EXPERT_FF // 2
    acc = None
    for f in range(2):
        fs = slice(f * half, (f + 1) * half)
        a = jnp.dot(xe, wg_ref[0, :, fs], preferred_element_type=F32)
        b = jnp.dot(xe, wu_ref[0, :, fs], preferred_element_type=F32)
        he = (a * _sigmoid(a) * b).astype(BF16)
        part = jnp.dot(he, wd_ref[0, fs, :], preferred_element_type=F32)
        acc = part if acc is None else acc + part
    o_ref[...] = (acc * g_ref[0]).astype(BF16)


def _moe_ffn(hp, idx, gate, wg, wu, wd):
    E, cap = idx.shape
    tm = min(TM_FFN, cap)
    assert tm % DMA_UNROLL == 0
    nt = cap // tm
    n_steps = E * nt
    half = D_MODEL // 2
    idx3 = idx.reshape(n_steps, 1, tm).astype(jnp.int32)
    g3 = gate.reshape(n_steps, tm, 1).astype(F32)
    vmem = (2 * (3 * D_MODEL * EXPERT_FF * 2 + tm * LANES * 4 + tm * D_MODEL * 2) + 2 * tm * half * 4
            + 3 * tm * EXPERT_FF * 4)
    smem = lambda f: pl.BlockSpec((1, 1, tm), f, memory_space=pltpu.SMEM)
    wspec = lambda shape: pl.BlockSpec((1,) + shape, lambda s: (s // nt, 0, 0))
    return pl.pallas_call(
        functools.partial(_moe_kernel, n_steps=n_steps),
        grid=(n_steps,),
        in_specs=[
            smem(lambda s: (s, 0, 0)),
            smem(lambda s: (jnp.minimum(s + 1, n_steps - 1), 0, 0)),
            pl.BlockSpec((1, tm, 1), lambda s: (s, 0, 0)),
            wspec((D_MODEL, EXPERT_FF)), wspec((D_MODEL, EXPERT_FF)), wspec((EXPERT_FF, D_MODEL)),
            pl.BlockSpec(memory_space=pl.ANY),
        ],
        out_specs=pl.BlockSpec((tm, D_MODEL), lambda s: (s, 0)),
        out_shape=jax.ShapeDtypeStruct((E * cap, D_MODEL), BF16),
        scratch_shapes=[pltpu.VMEM((2, tm, half), jnp.uint32), pltpu.SemaphoreType.DMA((2,))],
        compiler_params=pltpu.CompilerParams(
            dimension_semantics=("arbitrary",), vmem_limit_bytes=_vmem_limit(vmem)),
        name="moe_ffn",
    )(idx3, idx3, g3, wg, wu, wd, hp)


ROUTE_SLOT_CHUNK = 1024


def _select_kernel(a_ref, sel_ref, *, cap):
    keys = lax.bitcast_convert_type(a_ref[...], jnp.int32)
    n_exp, n_tok = keys.shape
    capf = jnp.float32(cap)

    def count(pred):
        return jnp.sum(pred.astype(F32), axis=1, keepdims=True)

    def value_bit(b, lo):
        cand = lo | jnp.left_shift(jnp.int32(1), 30 - b)
        return jnp.where(count(keys >= cand) >= capf, cand, lo)

    thr = lax.fori_loop(0, 31, value_bit, jnp.zeros((n_exp, 1), jnp.int32))
    gt = keys > thr
    eq = keys == thr
    need = capf - count(gt)
    tok = lax.broadcasted_iota(jnp.int32, keys.shape, 1)
    nbits = int(n_tok).bit_length()

    def index_bit(b, m):
        cand = m | jnp.left_shift(jnp.int32(1), nbits - 1 - b)
        ok = (cand <= n_tok) & (count(eq & (tok < cand)) <= need)
        return jnp.where(ok, cand, m)

    m = lax.fori_loop(0, nbits, index_bit, jnp.zeros((n_exp, 1), jnp.int32))
    sel_ref[...] = (gt | (eq & (tok < m))).astype(F32)


def _compact_kernel(sel_ref, aff_ref, idx_ref, gate_ref, pos_ref, bex_ref, *, cap, n_tok):
    sel = sel_ref[0]
    nb = sel.shape[0]
    ch = min(ROUTE_SLOT_CHUNK, cap)
    nt_dims = (((1,), (1,)), ((), ()))
    r128 = lax.broadcasted_iota(jnp.int32, (LANES, LANES), 0)
    c128 = lax.broadcasted_iota(jnp.int32, (LANES, LANES), 1)
    rj = lax.broadcasted_iota(jnp.int32, (nb, nb), 0)
    cj = lax.broadcasted_iota(jnp.int32, (nb, nb), 1)
    selb = sel.astype(BF16)
    cin = jnp.dot(selb, (r128 <= c128).astype(BF16), preferred_element_type=F32)
    totb = jnp.broadcast_to(cin[:, LANES - 1:LANES], (nb, LANES)).astype(BF16)
    bex = jnp.dot((cj < rj).astype(BF16), totb, preferred_element_type=F32)
    tot_row = lax.dot_general(jnp.ones((8, LANES), BF16), selb, nt_dims, preferred_element_type=F32)
    binc_row = jnp.dot(tot_row.astype(BF16), (rj <= cj).astype(BF16), preferred_element_type=F32)[0:1]
    pos_ref[0] = jnp.where(sel > 0.5, bex + cin - 1.0, -1.0)
    bex_ref[0] = bex[:, 0:1]
    bex_hi = jnp.floor(bex * (1.0 / 256.0))
    bex_lo = bex - 256.0 * bex_hi
    a_hi, a_mid, a_lo = _split3(aff_ref[0])
    table = jnp.concatenate([cin.astype(BF16), bex_hi.astype(BF16), bex_lo.astype(BF16), a_hi, a_mid, a_lo], axis=1)
    lane_nb = lax.broadcasted_iota(jnp.int32, (ch, nb), 1).astype(F32)
    lane128 = lax.broadcasted_iota(jnp.int32, (ch, LANES), 1).astype(F32)
    for c in range(cap // ch):
        s = (lax.broadcasted_iota(jnp.int32, (ch, 1), 0) + c * ch).astype(F32)
        blk = jnp.sum((binc_row <= s).astype(F32), axis=1, keepdims=True)
        onehot = (lane_nb == blk).astype(BF16)
        g = jnp.dot(onehot, table, preferred_element_type=F32)
        cinrow = g[:, 0:LANES]
        before = g[:, LANES:2 * LANES] * 256.0 + g[:, 2 * LANES:3 * LANES]
        affrow = g[:, 3 * LANES:4 * LANES] + g[:, 4 * LANES:5 * LANES] + g[:, 5 * LANES:6 * LANES]
        off = jnp.sum((cinrow <= s - before).astype(F32), axis=1, keepdims=True)
        idx = jnp.minimum(blk * float(LANES) + off, float(n_tok - 1))
        gate = jnp.sum(jnp.where(lane128 == off, affrow, 0.0), axis=1, keepdims=True)
        idx_t = jnp.transpose(jnp.broadcast_to(idx, (ch, LANES)))[0:1]
        gate_t = jnp.transpose(jnp.broadcast_to(gate, (ch, LANES)))[0:1]
        idx_ref[0, :, c * ch:(c + 1) * ch] = idx_t.astype(jnp.int32)
        gate_ref[0, :, c * ch:(c + 1) * ch] = gate_t


def _route(aff, cap):
    n_tok = aff.shape[0]
    nb = n_tok // LANES
    a_t = aff[:, :N_EXPERTS].T
    sel = pl.pallas_call(
        functools.partial(_select_kernel, cap=cap),
        out_shape=jax.ShapeDtypeStruct((N_EXPERTS, n_tok), F32),
        compiler_params=pltpu.CompilerParams(vmem_limit_bytes=_vmem_limit(12 * N_EXPERTS * n_tok * 4)),
        name="route_select",
    )(a_t)
    ch = min(ROUTE_SLOT_CHUNK, cap)
    vmem = 2 * (2 * nb * LANES * 4 + 2 * cap * 4) + 2 * nb * nb * 4 + nb * 6 * LANES * 4 + ch * (nb + 12 * LANES) * 4
    blk = pl.BlockSpec((1, nb, LANES), lambda e: (e, 0, 0))
    oblk = pl.BlockSpec((1, 1, cap), lambda e: (e, 0, 0))
    idx, gate, pos, bex = pl.pallas_call(
        functools.partial(_compact_kernel, cap=cap, n_tok=n_tok),
        grid=(N_EXPERTS,),
        in_specs=[blk, blk],
        out_specs=[oblk, oblk, blk, pl.BlockSpec((1, nb, 1), lambda e: (e, 0, 0))],
        out_shape=[jax.ShapeDtypeStruct((N_EXPERTS, 1, cap), jnp.int32),
                   jax.ShapeDtypeStruct((N_EXPERTS, 1, cap), F32),
                   jax.ShapeDtypeStruct((N_EXPERTS, nb, LANES), F32),
                   jax.ShapeDtypeStruct((N_EXPERTS, nb, 1), F32)],
        compiler_params=pltpu.CompilerParams(
            dimension_semantics=("arbitrary",), vmem_limit_bytes=_vmem_limit(vmem)),
        name="route_compact",
    )(sel.reshape(N_EXPERTS, nb, LANES), a_t.reshape(N_EXPERTS, nb, LANES))
    return (idx.reshape(N_EXPERTS, cap), gate.reshape(N_EXPERTS, cap), pos.reshape(N_EXPERTS, n_tok),
            bex.reshape(N_EXPERTS, nb))


COMBINE_WINDOW = 128
COMBINE_STRIDE = COMBINE_WINDOW - BF16_SUBLANES


def _ple_kernel(lo_ref, cnt_ref, x_ref, pos_ref, p_ref, np_ref, wg_ref, wp_ref, fn_ref, out_hbm, o_ref,
                wbuf, xbuf, sem, sem_x, *, final, cap, n_tiles):
    i = pl.program_id(0)
    slot = i % 2
    tm = x_ref.shape[0]
    W = COMBINE_WINDOW
    n_rows = N_EXPERTS * cap

    def win_start(row):
        aligned = jnp.left_shift(jnp.right_shift(row, 4), 4)
        return pl.multiple_of(jnp.minimum(aligned, n_rows - W), BF16_SUBLANES)

    def issue(tile, sl):
        for e in range(N_EXPERTS):
            st = win_start(e * cap + lo_ref[tile * N_EXPERTS + e])
            pltpu.make_async_copy(out_hbm.at[pl.ds(st, W)], wbuf.at[sl, pl.ds(e * W, W)], sem.at[sl]).start()

    @pl.when(i == 0)
    def _():
        issue(0, 0)

    @pl.when(i + 1 < n_tiles)
    def _():
        issue(i + 1, 1 - slot)

    pltpu.make_async_copy(out_hbm.at[pl.ds(0, N_EXPERTS * W)], wbuf.at[slot], sem.at[slot]).wait()

    posv = pos_ref[...]
    lane = lax.broadcasted_iota(jnp.int32, (tm, W), 1).astype(F32)
    parts = []
    for e in range(N_EXPERTS):
        st = win_start(e * cap + lo_ref[i * N_EXPERTS + e])
        rel = posv[:, e:e + 1] - (st - e * cap).astype(F32)
        parts.append((rel == lane).astype(BF16))
    y = jnp.dot(jnp.concatenate(parts, axis=1), wbuf[slot], preferred_element_type=F32)

    for e in range(N_EXPERTS):
        first = lo_ref[i * N_EXPERTS + e]
        covered_end = win_start(e * cap + first) - e * cap + W
        remaining = jnp.maximum(first + cnt_ref[i * N_EXPERTS + e] - covered_end, 0)
        n_extra = (remaining + COMBINE_STRIDE - 1) // COMBINE_STRIDE
        pe = posv[:, e:e + 1]

        def extra(k, acc, e=e, covered_end=covered_end, pe=pe):
            want = covered_end + k * COMBINE_STRIDE
            st = win_start(e * cap + want)
            cp = pltpu.make_async_copy(out_hbm.at[pl.ds(st, W)], xbuf, sem_x)
            cp.start()
            cp.wait()
            hit = (pe - (st - e * cap).astype(F32) == lane) & (pe >= want.astype(F32)) \
                & (pe < (want + COMBINE_STRIDE).astype(F32))
            return acc + jnp.dot(hit.astype(BF16), xbuf[...], preferred_element_type=F32)

        y = lax.fori_loop(0, n_extra, extra, y)

    x = x_ref[...] + y
    ms = jnp.mean(x * x, axis=-1, keepdims=True)
    h = (x * lax.rsqrt(ms + EPS) * np_ref[...]).astype(BF16)
    gate = _sigmoid(jnp.dot(h, wg_ref[...], preferred_element_type=F32))
    pp = jnp.dot(p_ref[...].astype(BF16), wp_ref[...], preferred_element_type=F32)
    x = x + gate * pp
    if final:
        ms = jnp.mean(x * x, axis=-1, keepdims=True)
        x = x * lax.rsqrt(ms + EPS) * fn_ref[...]
    o_ref[...] = x


def _ple(x2d, out, pos, bex, p2d, norm_ple, wg, wp, final_norm, final):
    T = x2d.shape[0]
    tm = min(TM_PLE, T)
    n_tiles = T // tm
    cap = out.shape[0] // N_EXPERTS
    W = COMBINE_WINDOW
    assert cap >= W and cap % BF16_SUBLANES == 0 and tm % LANES == 0
    lo = bex[:, ::tm // LANES].astype(jnp.int32)
    cnt = jnp.diff(lo, axis=1, append=jnp.full((N_EXPERTS, 1), cap, jnp.int32))
    lo = lo.T.reshape(-1)
    cnt = cnt.T.reshape(-1)
    pos_t = pos.T
    vmem = (2 * (2 * tm * D_MODEL * 4 + tm * LANES * 4 + tm * PLE_DIM * 4 + D_MODEL * D_MODEL * 2
                 + PLE_DIM * D_MODEL * 2) + 2 * N_EXPERTS * W * D_MODEL * 2 + W * D_MODEL * 2
            + tm * N_EXPERTS * W * 2 + 6 * tm * D_MODEL * 4)
    tok = lambda width: pl.BlockSpec((tm, width), lambda i, lo, cnt: (i, 0))
    full = lambda shape: pl.BlockSpec(shape, lambda i, lo, cnt: (0,) * len(shape))
    grid_spec = pltpu.PrefetchScalarGridSpec(
        num_scalar_prefetch=2,
        grid=(n_tiles,),
        in_specs=[tok(D_MODEL), tok(N_EXPERTS), tok(PLE_DIM), full((1, D_MODEL)), full((D_MODEL, D_MODEL)),
                  full((PLE_DIM, D_MODEL)), full((1, D_MODEL)), pl.BlockSpec(memory_space=pl.ANY)],
        out_specs=tok(D_MODEL),
        scratch_shapes=[
            pltpu.VMEM((2, N_EXPERTS * W, D_MODEL), BF16),
            pltpu.VMEM((W, D_MODEL), BF16),
            pltpu.SemaphoreType.DMA((2,)),
            pltpu.SemaphoreType.DMA,
        ],
    )
    return pl.pallas_call(
        functools.partial(_ple_kernel, final=final, cap=cap, n_tiles=n_tiles),
        grid_spec=grid_spec,
        out_shape=jax.ShapeDtypeStruct((T, D_MODEL), F32),
        compiler_params=pltpu.CompilerParams(
            dimension_semantics=("arbitrary",), vmem_limit_bytes=_vmem_limit(vmem)),
        name="combine_ple",
    )(lo, cnt, x2d, pos_t, p2d, norm_ple, wg, wp, final_norm, out)


def _prep_layer(i, norm_mix, w_in, conv_w, na_rpb, gla_w2, gla_b, gla_norm, w_branch, w_out, norm_ffn, w_router,
                w_exp_gate, w_exp_up, w_exp_down, norm_ple, w_ple_gate, w_ple_proj):
    w = w_in[i]
    sizes = (CONV_WIDTH,) * 3 + (NA_WIDTH,) * 3 + (GLA_KEY_WIDTH,) * 2 + (GLA_VAL_WIDTH,) * 2 + (2 * GLA_RANK,)
    offs = np.concatenate([[0], np.cumsum(sizes)])
    gates_off = int(offs[-1])
    lr_off = int(offs[10])
    w_main = jnp.concatenate([w[:, gates_off:], w[:, :lr_off]], axis=1)
    assert float(np.log2(NA_HEAD_DIM ** -0.5)).is_integer() and float(np.log2(GLA_DK ** -0.5)).is_integer()
    col_scale = np.ones((PROJ_COLS,), np.float32)
    col_scale[COL_NQ:COL_NQ + NA_WIDTH] = NA_HEAD_DIM ** -0.5
    col_scale[COL_GQ:COL_GQ + GLA_KEY_WIDTH] = GLA_DK ** -0.5
    w_main = (w_main * col_scale).astype(BF16)
    w_lr = jnp.zeros((D_MODEL, LANES), F32).at[:, :2 * GLA_RANK].set(w[:, lr_off:gates_off]).astype(BF16)
    return dict(
        norm_mix=norm_mix[i].reshape(1, D_MODEL), w_main=w_main, w_lr=w_lr, conv_w=conv_w[i], na_rpb=na_rpb[i],
        gla_w2=gla_w2[i], gla_b=gla_b[i], gla_norm=gla_norm[i], w_branch=w_branch[i].astype(BF16),
        w_out=w_out[i].astype(BF16), norm_ffn=norm_ffn[i].reshape(1, D_MODEL), w_router=w_router[i],
        w_exp_gate=w_exp_gate[i].astype(BF16), w_exp_up=w_exp_up[i].astype(BF16),
        w_exp_down=w_exp_down[i].astype(BF16), norm_ple=norm_ple[i].reshape(1, D_MODEL),
        w_ple_gate=w_ple_gate[i].astype(BF16), w_ple_proj=w_ple_proj[i].astype(BF16))


def _moe(hp, aff, lp):
    n_tok = hp.shape[0]
    cap = CAPACITY_FACTOR * n_tok // N_EXPERTS
    idx, gate_val, pos, bex = _route(aff, cap)
    out = _moe_ffn(hp, idx, gate_val, lp["w_exp_gate"], lp["w_exp_up"], lp["w_exp_down"])
    return out, pos, bex


def _trunk(x, p, layers, final_norm):
    bsz, L, _ = x.shape
    T = bsz * L
    x2d = x.reshape(T, D_MODEL)
    fn = final_norm.reshape(1, D_MODEL)
    for i, lp in enumerate(layers):
        proj, lr = _inproj(x2d, lp["norm_mix"], lp["w_main"], lp["w_lr"])
        y_b = _na_attention(proj, lp["na_rpb"], bsz, L)
        o_f, o_b = _gla(proj, lr, lp["gla_w2"], lp["gla_b"], bsz, L)
        x2d, h, aff = _merge(x2d, proj, y_b, o_f, o_b, lp["gla_norm"], lp["conv_w"], lp["w_branch"], lp["w_out"], lp["norm_ffn"],
                             lp["w_router"], L)
        out, pos, bex = _moe(h, aff, lp)
        x2d = _ple(x2d, out, pos, bex, p[i].reshape(T, PLE_DIM), lp["norm_ple"], lp["w_ple_gate"], lp["w_ple_proj"], fn,
                   final=(i == len(layers) - 1))
    return x2d.reshape(bsz, L, D_MODEL)


def kernel(x_prompt, x_sample, p_prompt, p_sample, norm_mix, w_in, conv_w, na_rpb, gla_w2, gla_b, gla_norm, w_branch, w_out, norm_ffn, w_router, w_exp_gate, w_exp_up, w_exp_down, norm_ple, w_ple_gate, w_ple_proj, final_norm):
    depth = w_in.shape[0]
    layers = [_prep_layer(i, norm_mix, w_in, conv_w, na_rpb, gla_w2, gla_b, gla_norm, w_branch, w_out, norm_ffn,
                          w_router, w_exp_gate, w_exp_up, w_exp_down, norm_ple, w_ple_gate, w_ple_proj)
              for i in range(depth)]
    y_prompt = _trunk(x_prompt, p_prompt, layers, final_norm)
    y_sample = _trunk(x_sample, p_sample, layers, final_norm)
    return (y_prompt, y_sample)
```

```python
import functools

import numpy as np
import jax
import jax.numpy as jnp
from jax import lax
from jax.experimental import pallas as pl
from jax.experimental.pallas import tpu as pltpu

F32 = jnp.float32
BF16 = jnp.bfloat16

D_MODEL = 1024
GRID_W = 64
PLE_DIM = 256
EPS = 1e-6
CONV_WIDTH = 512
CONV_K = 3
NA_HEADS = 8
NA_HEAD_DIM = 64
NA_WIDTH = NA_HEADS * NA_HEAD_DIM
WIN_ROWS = 8
WIN_COLS = 16
GLA_HEADS = 4
GLA_DK = 64
GLA_DV = 128
GLA_KEY_WIDTH = GLA_HEADS * GLA_DK
GLA_VAL_WIDTH = GLA_HEADS * GLA_DV
GLA_RANK = 16
GLA_GATE_NORM = 16.0
GLA_CHUNK = 64
N_BRANCH = 3
N_EXPERTS = 16
EXPERT_FF = 2048
CAPACITY_FACTOR = 2

V7X_VMEM_BYTES = 64 * 1024 * 1024
LANES = 128
BF16_SUBLANES = 16

PROJ_TN = 512
COL_GATES = 0
COL_AB = 3 * D_MODEL
COL_AC = COL_AB + CONV_WIDTH
COL_AX = COL_AC + CONV_WIDTH
COL_NQ = COL_AX + CONV_WIDTH
COL_NK = COL_NQ + NA_WIDTH
COL_NV = COL_NK + NA_WIDTH
COL_GQ = COL_NV + NA_WIDTH
COL_GK = COL_GQ + GLA_KEY_WIDTH
COL_GV = COL_GK + GLA_KEY_WIDTH
COL_GG = COL_GV + GLA_VAL_WIDTH
PROJ_COLS = COL_GG + GLA_VAL_WIDTH
N_GATE_BLOCKS = (3 * D_MODEL) // PROJ_TN

TM_INPROJ = 512
NA_TILE_ROWS = 4
NA_TQ = NA_TILE_ROWS * GRID_W
GLA_TC = 512
TM_MERGE = 256
TM_FFN = 512
TM_PLE = 512

NEG_INF = -1e30


def _vmem_limit(nbytes):
    return int(min(V7X_VMEM_BYTES - 6 * 1024 * 1024, max(32 * 1024 * 1024, nbytes + 8 * 1024 * 1024)))


def _sigmoid(x):
    return 1.0 / (1.0 + jnp.exp(-x))


def _split3(a):
    hi = a.astype(BF16)
    r1 = a - hi.astype(F32)
    mid = r1.astype(BF16)
    lo = (r1 - mid.astype(F32)).astype(BF16)
    return hi, mid, lo


def _inproj_kernel(x_ref, nw_ref, wlr_ref, w_hbm, proj_ref, lr_ref, w_vmem, sem):
    @pl.when(pl.program_id(0) == 0)
    def _():
        cp = pltpu.make_async_copy(w_hbm, w_vmem, sem)
        cp.start()
        cp.wait()

    x = x_ref[...]
    ms = jnp.mean(x * x, axis=-1, keepdims=True)
    hb = (x * lax.rsqrt(ms + EPS) * nw_ref[...]).astype(BF16)
    lr_ref[...] = jnp.dot(hb, wlr_ref[...], preferred_element_type=F32)
    for j in range(PROJ_COLS // PROJ_TN):
        cs = slice(j * PROJ_TN, (j + 1) * PROJ_TN)
        acc = jnp.dot(hb, w_vmem[:, cs], preferred_element_type=F32)
        if j < N_GATE_BLOCKS:
            acc = _sigmoid(acc)
        proj_ref[:, cs] = acc.astype(BF16)


def _inproj(x2d, norm_w, w_main, w_lr):
    T = x2d.shape[0]
    tm = min(TM_INPROJ, T)
    vmem = (2 * (tm * D_MODEL * 4 + D_MODEL * LANES * 2 + tm * PROJ_COLS * 2 + tm * LANES * 4)
            + D_MODEL * PROJ_COLS * 2 + tm * D_MODEL * 2 + 4 * tm * PROJ_TN * 4)
    return pl.pallas_call(
        _inproj_kernel,
        grid=(T // tm,),
        in_specs=[
            pl.BlockSpec((tm, D_MODEL), lambda i: (i, 0)),
            pl.BlockSpec((1, D_MODEL), lambda i: (0, 0)),
            pl.BlockSpec((D_MODEL, LANES), lambda i: (0, 0)),
            pl.BlockSpec(memory_space=pl.ANY),
        ],
        out_specs=[
            pl.BlockSpec((tm, PROJ_COLS), lambda i: (i, 0)),
            pl.BlockSpec((tm, LANES), lambda i: (i, 0)),
        ],
        out_shape=[
            jax.ShapeDtypeStruct((T, PROJ_COLS), BF16),
            jax.ShapeDtypeStruct((T, LANES), F32),
        ],
        scratch_shapes=[pltpu.VMEM((D_MODEL, PROJ_COLS), BF16), pltpu.SemaphoreType.DMA],
        compiler_params=pltpu.CompilerParams(
            dimension_semantics=("arbitrary",), vmem_limit_bytes=_vmem_limit(vmem)),
        name="inproj",
    )(x2d, norm_w, w_lr, w_main)


def _na_tables(rows):
    R = NA_TILE_ROWS
    J = rows // R
    kr = min(WIN_ROWS, rows)
    qr_rel = np.arange(NA_TQ) // GRID_W
    qc = np.arange(NA_TQ) % GRID_W
    kk = np.arange(3 * NA_TQ)
    kblk = kk // NA_TQ
    kwithin = kk % NA_TQ
    kc = kwithin % GRID_W
    col_start = np.clip(qc - WIN_COLS // 2, 0, GRID_W - WIN_COLS)
    col_ok = (kc[None, :] >= col_start[:, None]) & (kc[None, :] < col_start[:, None] + WIN_COLS)
    dc = np.clip(kc[None, :] - qc[:, None] + WIN_COLS - 1, 0, 2 * WIN_COLS - 2)
    types, keys, tile_type = [], {}, []
    for j in range(J):
        blocks = np.array([max(j - 1, 0), j, min(j + 1, J - 1)])
        blk_ok = np.array([j - 1 >= 0, True, j + 1 <= J - 1])
        krow = blocks[kblk] * R + kwithin // GRID_W
        qrow = j * R + qr_rel
        rs = np.clip(qrow - kr // 2, 0, rows - kr)
        row_ok = (krow[None, :] >= rs[:, None]) & (krow[None, :] < rs[:, None] + kr) & blk_ok[kblk][None, :]
        mask = row_ok & col_ok
        dr = np.clip(krow[None, :] - qrow[:, None] + WIN_ROWS - 1, 0, 2 * WIN_ROWS - 2)
        dr = np.where(mask, dr, 0)
        key = mask.tobytes() + dr.astype(np.int8).tobytes()
        if key not in keys:
            keys[key] = len(types)
            types.append((mask, dr, np.where(mask, dc, 0)))
        tile_type.append(keys[key])
    mask = np.stack([t[0] for t in types])
    dr = np.stack([t[1] for t in types]).astype(np.int32)
    dcs = np.stack([t[2] for t in types]).astype(np.int32)
    return mask, dr, dcs, np.array(tile_type, np.int32)


def _na_bias(rpb, mask, dr):
    n_types = mask.shape[0]
    W = GRID_W
    qc = np.arange(W)[:, None]
    kc = np.arange(W)[None, :]
    col_start = np.clip(qc - WIN_COLS // 2, 0, W - WIN_COLS)
    col_ok = (kc >= col_start) & (kc < col_start + WIN_COLS)
    n_dc = 2 * WIN_COLS - 1
    lead = W - WIN_COLS
    padded = jnp.pad(rpb, ((0, 0), (0, 0), (lead, 2 * W - 1 - lead - n_dc)), constant_values=NEG_INF)
    toep = jnp.stack([padded[:, :, W - 1 - q:2 * W - 1 - q] for q in range(W)], axis=2)
    toep = jnp.where(col_ok[None, None], toep, NEG_INF)
    dead = jnp.full((NA_HEADS, 1, W, W), NEG_INF, F32)
    blocks = jnp.concatenate([toep, dead], axis=1)
    n_dead = toep.shape[1]
    tables = []
    for t in range(n_types):
        ids = np.where(mask[t, ::W, ::W], dr[t, ::W, ::W], n_dead)
        rows_ = [jnp.concatenate([blocks[:, int(b)] for b in ids_row], axis=-1) for ids_row in ids]
        tables.append(jnp.concatenate(rows_, axis=1))
    return jnp.stack(tables)


def _na_kernel(tid_ref, q_ref, kp_ref, kc_ref, kn_ref, vp_ref, vc_ref, vn_ref, bias_ref, o_ref):
    del tid_ref
    q = q_ref[...]
    k = jnp.concatenate([kp_ref[...], kc_ref[...], kn_ref[...]], axis=0)
    v = jnp.concatenate([vp_ref[...], vc_ref[...], vn_ref[...]], axis=0)
    outs = []
    for h in range(NA_HEADS):
        sl = slice(h * NA_HEAD_DIM, (h + 1) * NA_HEAD_DIM)
        s = lax.dot_general(q[:, sl], k[:, sl], (((1,), (1,)), ((), ())), preferred_element_type=F32)
        s = s + bias_ref[0, h]
        m = jnp.max(s, axis=-1, keepdims=True)
        p = jnp.exp(s - m)
        l = jnp.sum(p, axis=-1, keepdims=True)
        o = jnp.dot(p.astype(BF16), v[:, sl], preferred_element_type=F32)
        outs.append(o * (1.0 / l))
    o_ref[...] = jnp.concatenate(outs, axis=-1).astype(BF16)


def _na_attention(proj, rpb, bsz, L):
    T = proj.shape[0]
    rows = L // GRID_W
    J = rows // NA_TILE_ROWS
    mask, dr, _, tile_type = _na_tables(rows)
    bias = _na_bias(rpb.astype(F32), mask, dr)
    qb, kb, vb = COL_NQ // NA_WIDTH, COL_NK // NA_WIDTH, COL_NV // NA_WIDTH

    def prev(b, j, tid):
        return b * J + jnp.maximum(j - 1, 0)

    def nxt(b, j, tid):
        return b * J + jnp.minimum(j + 1, J - 1)

    blk = (NA_TQ, NA_WIDTH)
    vmem = 2 * (8 * NA_TQ * NA_WIDTH * 2 + NA_HEADS * NA_TQ * 3 * NA_TQ * 4) + 8 * NA_TQ * 3 * NA_TQ * 4
    grid_spec = pltpu.PrefetchScalarGridSpec(
        num_scalar_prefetch=1,
        grid=(bsz, J),
        in_specs=[
            pl.BlockSpec(blk, lambda b, j, tid: (b * J + j, qb)),
            pl.BlockSpec(blk, lambda b, j, tid: (prev(b, j, tid), kb)),
            pl.BlockSpec(blk, lambda b, j, tid: (b * J + j, kb)),
            pl.BlockSpec(blk, lambda b, j, tid: (nxt(b, j, tid), kb)),
            pl.BlockSpec(blk, lambda b, j, tid: (prev(b, j, tid), vb)),
            pl.BlockSpec(blk, lambda b, j, tid: (b * J + j, vb)),
            pl.BlockSpec(blk, lambda b, j, tid: (nxt(b, j, tid), vb)),
            pl.BlockSpec((1, NA_HEADS, NA_TQ, 3 * NA_TQ), lambda b, j, tid: (tid[j], 0, 0, 0)),
        ],
        out_specs=pl.BlockSpec(blk, lambda b, j, tid: (b * J + j, 0)),
    )
    return pl.pallas_call(
        _na_kernel,
        grid_spec=grid_spec,
        out_shape=jax.ShapeDtypeStruct((T, NA_WIDTH), BF16),
        compiler_params=pltpu.CompilerParams(
            dimension_semantics=("arbitrary", "arbitrary"), vmem_limit_bytes=_vmem_limit(vmem)),
        name="na_attention",
    )(jnp.asarray(tile_type), proj, proj, proj, proj, proj, proj, proj, bias)


def _gla_kernel(q_ref, k_ref, v_ref, gg_ref, lr_ref, cum_ref, w2_ref, gb_ref, gn_ref, o_ref, ofwd_scr, st_scr,
                *, nt):
    p = pl.program_id(1)
    j = pl.program_id(2)
    c = GLA_CHUNK
    n_chunks = GLA_TC // c
    H = GLA_HEADS

    @pl.when(j == 0)
    def _():
        st_scr[...] = jnp.zeros_like(st_scr)

    lane_head = lax.broadcasted_iota(jnp.int32, (1, GLA_KEY_WIDTH), 1) // GLA_DK
    ri = lax.broadcasted_iota(jnp.int32, (H * c, c), 0) % c
    ci = lax.broadcasted_iota(jnp.int32, (H * c, c), 1)

    def sweep(direction):
        tri = (ci <= ri) if direction == 0 else (ci >= ri)
        z = lr_ref[...].astype(BF16)
        xg = jnp.dot(z, w2_ref[direction], preferred_element_type=F32) + gb_ref[direction]
        la = (jnp.minimum(xg, 0.0) - jnp.log1p(jnp.exp(-jnp.abs(xg)))) * (1.0 / GLA_GATE_NORM)
        hi, mid, lo = _split3(la)
        cum = cum_ref[direction]
        b_all = (jnp.dot(cum, hi, preferred_element_type=F32)
                 + jnp.dot(cum, mid, preferred_element_type=F32)
                 + jnp.dot(cum, lo, preferred_element_type=F32))
        k_all = k_ref[...].astype(F32)
        qd_all = q_ref[...].astype(F32) * jnp.exp(b_all)
        kd_all = (k_all * jnp.exp(-b_all)).astype(BF16)
        st = st_scr[...]
        order = range(n_chunks) if direction == 0 else range(n_chunks - 1, -1, -1)
        for ch in order:
            rs = slice(ch * c, (ch + 1) * c)
            b = b_all[rs]
            b_last = b[c - 1:c, :] if direction == 0 else b[0:1, :]
            v = v_ref[rs, :]
            qd = qd_all[rs]
            kd = kd_all[rs]
            krem = (k_all[rs] * jnp.exp(b_last - b)).astype(BF16)
            st_b = st.astype(BF16)
            nt_dims = (((1,), (1,)), ((), ()))
            qs = jnp.concatenate([jnp.where(lane_head == h, qd, 0.0) for h in range(H)], axis=0).astype(BF16)
            att = lax.dot_general(qs, kd, nt_dims, preferred_element_type=F32)
            att = jnp.where(tri, att, 0.0).astype(BF16)
            intra = jnp.dot(att, v, preferred_element_type=F32)
            inter = lax.dot_general(qs, st_b, nt_dims, preferred_element_type=F32)
            o_chunk = jnp.concatenate(
                [intra[h * c:(h + 1) * c, h * GLA_DV:(h + 1) * GLA_DV] + inter[h * c:(h + 1) * c]
                 for h in range(H)], axis=-1)
            kv = lax.dot_general(v, krem, (((0,), (0,)), ((), ())), preferred_element_type=F32)
            st = st * jnp.exp(b_last)
            for h in range(H):
                st = st + jnp.where(lane_head == h, kv[h * GLA_DV:(h + 1) * GLA_DV, :], 0.0)
            yield ch, o_chunk
        st_scr[...] = st

    @pl.when(p == 0)
    def _():
        for ch, o_chunk in sweep(0):
            start = pl.multiple_of(j * GLA_TC + ch * c, c)
            ofwd_scr[pl.ds(start, c), :] = o_chunk

    @pl.when(p == 1)
    def _():
        t = nt - 1 - j
        gn = gn_ref[...]
        for ch, o_chunk in sweep(1):
            start = pl.multiple_of(t * GLA_TC + ch * c, c)
            o = ofwd_scr[pl.ds(start, c), :] + o_chunk
            normed = []
            for h in range(H):
                oh = o[:, h * GLA_DV:(h + 1) * GLA_DV]
                ms = jnp.mean(oh * oh, axis=-1, keepdims=True)
                normed.append(oh * lax.rsqrt(ms + EPS))
            on = jnp.concatenate(normed, axis=-1) * gn
            g = gg_ref[ch * c:(ch + 1) * c, :].astype(F32)
            o_ref[ch * c:(ch + 1) * c, :] = (g * _sigmoid(g) * on).astype(BF16)


def _gla(proj, lr, gla_w2, gla_b, gla_norm, bsz, L):
    T = proj.shape[0]
    nt = L // GLA_TC
    w2p = jnp.zeros((2, LANES, GLA_KEY_WIDTH), F32)
    w2p = w2p.at[0, 0:GLA_RANK].set(gla_w2[0]).at[1, GLA_RANK:2 * GLA_RANK].set(gla_w2[1]).astype(BF16)
    gb = gla_b.reshape(2, 1, GLA_KEY_WIDTH).astype(F32)
    gn = gla_norm.reshape(1, GLA_VAL_WIDTH).astype(F32)
    r = np.arange(GLA_TC)
    same = (r[:, None] // GLA_CHUNK) == (r[None, :] // GLA_CHUNK)
    cum = jnp.asarray(np.stack([same & (r[None, :] <= r[:, None]), same & (r[None, :] >= r[:, None])]), BF16)

    def tile(b, p, j):
        return b * nt + jnp.where(p == 0, j, nt - 1 - j)

    def otile(b, p, j):
        return b * nt + jnp.where(p == 0, nt - 1, nt - 1 - j)

    kq, kk = COL_GQ // GLA_KEY_WIDTH, COL_GK // GLA_KEY_WIDTH
    kv, kg = COL_GV // GLA_VAL_WIDTH, COL_GG // GLA_VAL_WIDTH
    vmem = (2 * (2 * GLA_TC * GLA_KEY_WIDTH * 2 + 3 * GLA_TC * GLA_VAL_WIDTH * 2 + GLA_TC * LANES * 4
                 + 2 * GLA_TC * GLA_TC * 2)
            + L * GLA_VAL_WIDTH * 4 + GLA_DV * GLA_KEY_WIDTH * 4 + 16 * GLA_TC * GLA_KEY_WIDTH * 4)
    return pl.pallas_call(
        functools.partial(_gla_kernel, nt=nt),
        grid=(bsz, 2, nt),
        in_specs=[
            pl.BlockSpec((GLA_TC, GLA_KEY_WIDTH), lambda b, p, j: (tile(b, p, j), kq)),
            pl.BlockSpec((GLA_TC, GLA_KEY_WIDTH), lambda b, p, j: (tile(b, p, j), kk)),
            pl.BlockSpec((GLA_TC, GLA_VAL_WIDTH), lambda b, p, j: (tile(b, p, j), kv)),
            pl.BlockSpec((GLA_TC, GLA_VAL_WIDTH), lambda b, p, j: (tile(b, p, j), kg)),
            pl.BlockSpec((GLA_TC, LANES), lambda b, p, j: (tile(b, p, j), 0)),
            pl.BlockSpec((2, GLA_TC, GLA_TC), lambda b, p, j: (0, 0, 0)),
            pl.BlockSpec((2, LANES, GLA_KEY_WIDTH), lambda b, p, j: (0, 0, 0)),
            pl.BlockSpec((2, 1, GLA_KEY_WIDTH), lambda b, p, j: (0, 0, 0)),
            pl.BlockSpec((1, GLA_VAL_WIDTH), lambda b, p, j: (0, 0)),
        ],
        out_specs=pl.BlockSpec((GLA_TC, GLA_VAL_WIDTH), lambda b, p, j: (otile(b, p, j), 0)),
        out_shape=jax.ShapeDtypeStruct((T, GLA_VAL_WIDTH), BF16),
        scratch_shapes=[pltpu.VMEM((L, GLA_VAL_WIDTH), F32), pltpu.VMEM((GLA_DV, GLA_KEY_WIDTH), F32)],
        compiler_params=pltpu.CompilerParams(
            dimension_semantics=("arbitrary", "arbitrary", "arbitrary"), vmem_limit_bytes=_vmem_limit(vmem)),
        name="gla",
    )(proj, proj, proj, proj, lr, cum, w2p, gb, gn)


def _merge_kernel(x_ref, ga_ref, gb_ref, gc_ref, ab_ref, ac_ref, ax_ref, acp_ref, axp_ref, acn_ref, axn_ref,
                  yb_ref, yc_ref, cw_ref, wb_ref, wo_ref, nf_ref, wr_ref, xo_ref, h_ref, aff_ref, *, tiles_per_seq):
    i = pl.program_id(0)
    tm = x_ref.shape[0]
    pos = i % tiles_per_seq
    has_prev = jnp.where(pos == 0, 0.0, 1.0)
    has_next = jnp.where(pos == tiles_per_seq - 1, 0.0, 1.0)
    u = ac_ref[...].astype(F32) * ax_ref[...].astype(F32)
    up = (acp_ref[...].astype(F32) * axp_ref[...].astype(F32))[BF16_SUBLANES - 1:BF16_SUBLANES, :] * has_prev
    un = (acn_ref[...].astype(F32) * axn_ref[...].astype(F32))[0:1, :] * has_next
    ridx = lax.broadcasted_iota(jnp.int32, (tm, 1), 0)
    u_prev = jnp.where(ridx == 0, up, pltpu.roll(u, 1, axis=0))
    u_next = jnp.where(ridx == tm - 1, un, pltpu.roll(u, tm - 1, axis=0))
    cw = cw_ref[...]
    y_a = ab_ref[...].astype(F32) * (u_prev * cw[0:1, :] + u * cw[1:2, :] + u_next * cw[2:3, :])
    m = ga_ref[...].astype(F32) * jnp.dot(y_a.astype(BF16), wb_ref[0], preferred_element_type=F32)
    m = m + gb_ref[...].astype(F32) * jnp.dot(yb_ref[...], wb_ref[1], preferred_element_type=F32)
    m = m + gc_ref[...].astype(F32) * jnp.dot(yc_ref[...], wb_ref[2], preferred_element_type=F32)
    x_new = x_ref[...] + jnp.dot(m.astype(BF16), wo_ref[...], preferred_element_type=F32)
    xo_ref[...] = x_new
    ms = jnp.mean(x_new * x_new, axis=-1, keepdims=True)
    h = x_new * lax.rsqrt(ms + EPS) * nf_ref[...]
    bits = lax.bitcast_convert_type(h.astype(BF16).astype(F32), jnp.uint32)
    half = D_MODEL // 2
    h_ref[...] = (bits[:, :half] >> 16) | (bits[:, half:] & jnp.uint32(0xFFFF0000))
    h0, h1, h2 = _split3(h)
    w0, w1, w2 = wr_ref[0], wr_ref[1], wr_ref[2]
    logits = (jnp.dot(h0, w0, preferred_element_type=F32) + jnp.dot(h0, w1, preferred_element_type=F32)
              + jnp.dot(h1, w0, preferred_element_type=F32) + jnp.dot(h1, w1, preferred_element_type=F32)
              + jnp.dot(h0, w2, preferred_element_type=F32) + jnp.dot(h2, w0, preferred_element_type=F32))
    lane = lax.broadcasted_iota(jnp.int32, logits.shape, 1)
    logits = jnp.where(lane < N_EXPERTS, logits, NEG_INF)
    mx = jnp.max(logits, axis=-1, keepdims=True)
    e = jnp.exp(logits - mx)
    aff_ref[...] = e / jnp.sum(e, axis=-1, keepdims=True)


def _merge(x2d, proj, y_b, y_c, conv_w, wb, wo, norm_ffn, w_router, L):
    T = x2d.shape[0]
    tm = min(TM_MERGE, L)
    tiles_per_seq = L // tm
    n16 = T // BF16_SUBLANES
    per16 = tm // BF16_SUBLANES
    wr = jnp.zeros((D_MODEL, LANES), F32).at[:, :N_EXPERTS].set(w_router)
    wr = jnp.stack(_split3(wr))
    cb = CONV_WIDTH
    vmem = 2 * (2 * tm * D_MODEL * 4 + 3 * tm * D_MODEL * 2 + 5 * tm * cb * 2 + 4 * BF16_SUBLANES * cb * 2
                + 3 * cb * D_MODEL * 2 + D_MODEL * D_MODEL * 2 + 3 * D_MODEL * LANES * 2
                + tm * D_MODEL * 2 + tm * LANES * 4) + 8 * tm * D_MODEL * 4
    tok = lambda width, col: pl.BlockSpec((tm, width), lambda i: (i, col // width))
    halo_p = lambda col: pl.BlockSpec((BF16_SUBLANES, cb), lambda i: (jnp.maximum(i * per16 - 1, 0), col // cb))
    halo_n = lambda col: pl.BlockSpec((BF16_SUBLANES, cb),
                                      lambda i: (jnp.minimum((i + 1) * per16, n16 - 1), col // cb))
    full = lambda shape: pl.BlockSpec(shape, lambda i: (0,) * len(shape))
    return pl.pallas_call(
        functools.partial(_merge_kernel, tiles_per_seq=tiles_per_seq),
        grid=(T // tm,),
        in_specs=[
            tok(D_MODEL, 0),
            tok(D_MODEL, COL_GATES), tok(D_MODEL, COL_GATES + D_MODEL), tok(D_MODEL, COL_GATES + 2 * D_MODEL),
            tok(cb, COL_AB), tok(cb, COL_AC), tok(cb, COL_AX),
            halo_p(COL_AC), halo_p(COL_AX), halo_n(COL_AC), halo_n(COL_AX),
            tok(NA_WIDTH, 0), tok(GLA_VAL_WIDTH, 0),
            full((CONV_K, cb)), full((N_BRANCH, cb, D_MODEL)), full((D_MODEL, D_MODEL)),
            full((1, D_MODEL)), full((3, D_MODEL, LANES)),
        ],
        out_specs=[tok(D_MODEL, 0), tok(D_MODEL // 2, 0), tok(LANES, 0)],
        out_shape=[
            jax.ShapeDtypeStruct((T, D_MODEL), F32),
            jax.ShapeDtypeStruct((T, D_MODEL // 2), jnp.uint32),
            jax.ShapeDtypeStruct((T, LANES), F32),
        ],
        compiler_params=pltpu.CompilerParams(
            dimension_semantics=("arbitrary",), vmem_limit_bytes=_vmem_limit(vmem)),
        name="merge",
    )(x2d, proj, proj, proj, proj, proj, proj, proj, proj, proj, proj, y_b, y_c,
      conv_w, wb, wo, norm_ffn, wr)


DMA_UNROLL = 8


def _moe_kernel(idx_ref, idxn_ref, g_ref, wg_ref, wu_ref, wd_ref, hp_hbm, o_ref, hbuf, sem_h, *, n_steps):
    tm = hbuf.shape[1]
    s = pl.program_id(0)
    slot = s % 2

    def gather_rows(idx, sl):
        def body(g, c):
            for k in range(DMA_UNROLL):
                r = g * DMA_UNROLL + k
                t = idx[0, 0, r]
                pltpu.make_async_copy(hp_hbm.at[pl.ds(t, 1)], hbuf.at[sl, pl.ds(r, 1)],
                                      sem_h.at[sl]).start(priority=k % 2)
            return c
        lax.fori_loop(0, tm // DMA_UNROLL, body, 0)

    @pl.when(s == 0)
    def _():
        gather_rows(idx_ref, slot)

    @pl.when(s + 1 < n_steps)
    def _():
        gather_rows(idxn_ref, 1 - slot)

    pltpu.make_async_copy(hp_hbm.at[pl.ds(0, tm)], hbuf.at[slot], sem_h.at[slot]).wait()
    w = hbuf[slot]
    lo = lax.bitcast_convert_type(w << 16, F32)
    hi = lax.bitcast_convert_type(w & jnp.uint32(0xFFFF0000), F32)
    xe = jnp.concatenate([lo, hi], axis=1).astype(BF16)
    half = EXPERT_FF // 2
    acc = None
    for f in range(2):
        fs = slice(f * half, (f + 1) * half)
        a = jnp.dot(xe, wg_ref[0, :, fs], preferred_element_type=F32)
        b = jnp.dot(xe, wu_ref[0, :, fs], preferred_element_type=F32)
        he = (a * _sigmoid(a) * b).astype(BF16)
        part = jnp.dot(he, wd_ref[0, fs, :], preferred_element_type=F32)
        acc = part if acc is None else acc + part
    o_ref[...] = (acc * g_ref[0]).astype(BF16)


def _moe_ffn(hp, idx, gate, wg, wu, wd):
    E, cap = idx.shape
    tm = min(TM_FFN, cap)
    assert tm % DMA_UNROLL == 0
    nt = cap // tm
    n_steps = E * nt
    half = D_MODEL // 2
    idx3 = idx.reshape(n_steps, 1, tm).astype(jnp.int32)
    g3 = gate.reshape(n_steps, tm, 1).astype(F32)
    vmem = (2 * (3 * D_MODEL * EXPERT_FF * 2 + tm * LANES * 4 + tm * D_MODEL * 2) + 2 * tm * half * 4
            + 3 * tm * EXPERT_FF * 4)
    smem = lambda f: pl.BlockSpec((1, 1, tm), f, memory_space=pltpu.SMEM)
    wspec = lambda shape: pl.BlockSpec((1,) + shape, lambda s: (s // nt, 0, 0))
    return pl.pallas_call(
        functools.partial(_moe_kernel, n_steps=n_steps),
        grid=(n_steps,),
        in_specs=[
            smem(lambda s: (s, 0, 0)),
            smem(lambda s: (jnp.minimum(s + 1, n_steps - 1), 0, 0)),
            pl.BlockSpec((1, tm, 1), lambda s: (s, 0, 0)),
            wspec((D_MODEL, EXPERT_FF)), wspec((D_MODEL, EXPERT_FF)), wspec((EXPERT_FF, D_MODEL)),
            pl.BlockSpec(memory_space=pl.ANY),
        ],
        out_specs=pl.BlockSpec((tm, D_MODEL), lambda s: (s, 0)),
        out_shape=jax.ShapeDtypeStruct((E * cap, D_MODEL), BF16),
        scratch_shapes=[pltpu.VMEM((2, tm, half), jnp.uint32), pltpu.SemaphoreType.DMA((2,))],
        compiler_params=pltpu.CompilerParams(
            dimension_semantics=("arbitrary",), vmem_limit_bytes=_vmem_limit(vmem)),
        name="moe_ffn",
    )(idx3, idx3, g3, wg, wu, wd, hp)


ROUTE_SLOT_CHUNK = 1024


def _select_kernel(a_ref, sel_ref, *, cap):
    keys = lax.bitcast_convert_type(a_ref[...], jnp.int32)
    n_exp, n_tok = keys.shape
    capf = jnp.float32(cap)

    def count(pred):
        return jnp.sum(pred.astype(F32), axis=1, keepdims=True)

    def value_bit(b, lo):
        cand = lo | jnp.left_shift(jnp.int32(1), 30 - b)
        return jnp.where(count(keys >= cand) >= capf, cand, lo)

    thr = lax.fori_loop(0, 31, value_bit, jnp.zeros((n_exp, 1), jnp.int32))
    gt = keys > thr
    eq = keys == thr
    need = capf - count(gt)
    tok = lax.broadcasted_iota(jnp.int32, keys.shape, 1)
    nbits = int(n_tok).bit_length()

    def index_bit(b, m):
        cand = m | jnp.left_shift(jnp.int32(1), nbits - 1 - b)
        ok = (cand <= n_tok) & (count(eq & (tok < cand)) <= need)
        return jnp.where(ok, cand, m)

    m = lax.fori_loop(0, nbits, index_bit, jnp.zeros((n_exp, 1), jnp.int32))
    sel_ref[...] = (gt | (eq & (tok < m))).astype(F32)


def _compact_kernel(sel_ref, aff_ref, idx_ref, gate_ref, pos_ref, bex_ref, *, cap, n_tok):
    sel = sel_ref[0]
    nb = sel.shape[0]
    ch = min(ROUTE_SLOT_CHUNK, cap)
    nt_dims = (((1,), (1,)), ((), ()))
    r128 = lax.broadcasted_iota(jnp.int32, (LANES, LANES), 0)
    c128 = lax.broadcasted_iota(jnp.int32, (LANES, LANES), 1)
    rj = lax.broadcasted_iota(jnp.int32, (nb, nb), 0)
    cj = lax.broadcasted_iota(jnp.int32, (nb, nb), 1)
    selb = sel.astype(BF16)
    cin = jnp.dot(selb, (r128 <= c128).astype(BF16), preferred_element_type=F32)
    totb = jnp.broadcast_to(cin[:, LANES - 1:LANES], (nb, LANES)).astype(BF16)
    bex = jnp.dot((cj < rj).astype(BF16), totb, preferred_element_type=F32)
    tot_row = lax.dot_general(jnp.ones((8, LANES), BF16), selb, nt_dims, preferred_element_type=F32)
    binc_row = jnp.dot(tot_row.astype(BF16), (rj <= cj).astype(BF16), preferred_element_type=F32)[0:1]
    pos_ref[0] = jnp.where(sel > 0.5, bex + cin - 1.0, -1.0)
    bex_ref[0] = bex[:, 0:1]
    bex_hi = jnp.floor(bex * (1.0 / 256.0))
    bex_lo = bex - 256.0 * bex_hi
    a_hi, a_mid, a_lo = _split3(aff_ref[0])
    table = jnp.concatenate([cin.astype(BF16), bex_hi.astype(BF16), bex_lo.astype(BF16), a_hi, a_mid, a_lo], axis=1)
    lane_nb = lax.broadcasted_iota(jnp.int32, (ch, nb), 1).astype(F32)
    lane128 = lax.broadcasted_iota(jnp.int32, (ch, LANES), 1).astype(F32)
    for c in range(cap // ch):
        s = (lax.broadcasted_iota(jnp.int32, (ch, 1), 0) + c * ch).astype(F32)
        blk = jnp.sum((binc_row <= s).astype(F32), axis=1, keepdims=True)
        onehot = (lane_nb == blk).astype(BF16)
        g = jnp.dot(onehot, table, preferred_element_type=F32)
        cinrow = g[:, 0:LANES]
        before = g[:, LANES:2 * LANES] * 256.0 + g[:, 2 * LANES:3 * LANES]
        affrow = g[:, 3 * LANES:4 * LANES] + g[:, 4 * LANES:5 * LANES] + g[:, 5 * LANES:6 * LANES]
        off = jnp.sum((cinrow <= s - before).astype(F32), axis=1, keepdims=True)
        idx = jnp.minimum(blk * float(LANES) + off, float(n_tok - 1))
        gate = jnp.sum(jnp.where(lane128 == off, affrow, 0.0), axis=1, keepdims=True)
        idx_t = jnp.transpose(jnp.broadcast_to(idx, (ch, LANES)))[0:1]
        gate_t = jnp.transpose(jnp.broadcast_to(gate, (ch, LANES)))[0:1]
        idx_ref[0, :, c * ch:(c + 1) * ch] = idx_t.astype(jnp.int32)
        gate_ref[0, :, c * ch:(c + 1) * ch] = gate_t


def _route(aff, cap):
    n_tok = aff.shape[0]
    nb = n_tok // LANES
    a_t = aff[:, :N_EXPERTS].T
    sel = pl.pallas_call(
        functools.partial(_select_kernel, cap=cap),
        out_shape=jax.ShapeDtypeStruct((N_EXPERTS, n_tok), F32),
        compiler_params=pltpu.CompilerParams(vmem_limit_bytes=_vmem_limit(12 * N_EXPERTS * n_tok * 4)),
        name="route_select",
    )(a_t)
    ch = min(ROUTE_SLOT_CHUNK, cap)
    vmem = 2 * (2 * nb * LANES * 4 + 2 * cap * 4) + 2 * nb * nb * 4 + nb * 6 * LANES * 4 + ch * (nb + 12 * LANES) * 4
    blk = pl.BlockSpec((1, nb, LANES), lambda e: (e, 0, 0))
    oblk = pl.BlockSpec((1, 1, cap), lambda e: (e, 0, 0))
    idx, gate, pos, bex = pl.pallas_call(
        functools.partial(_compact_kernel, cap=cap, n_tok=n_tok),
        grid=(N_EXPERTS,),
        in_specs=[blk, blk],
        out_specs=[oblk, oblk, blk, pl.BlockSpec((1, nb, 1), lambda e: (e, 0, 0))],
        out_shape=[jax.ShapeDtypeStruct((N_EXPERTS, 1, cap), jnp.int32),
                   jax.ShapeDtypeStruct((N_EXPERTS, 1, cap), F32),
                   jax.ShapeDtypeStruct((N_EXPERTS, nb, LANES), F32),
                   jax.ShapeDtypeStruct((N_EXPERTS, nb, 1), F32)],
        compiler_params=pltpu.CompilerParams(
            dimension_semantics=("arbitrary",), vmem_limit_bytes=_vmem_limit(vmem)),
        name="route_compact",
    )(sel.reshape(N_EXPERTS, nb, LANES), a_t.reshape(N_EXPERTS, nb, LANES))
    return (idx.reshape(N_EXPERTS, cap), gate.reshape(N_EXPERTS, cap), pos.reshape(N_EXPERTS, n_tok),
            bex.reshape(N_EXPERTS, nb))


COMBINE_WINDOW = 128
COMBINE_STRIDE = COMBINE_WINDOW - BF16_SUBLANES


def _ple_kernel(lo_ref, cnt_ref, x_ref, pos_ref, p_ref, np_ref, wg_ref, wp_ref, fn_ref, out_hbm, o_ref,
                wbuf, xbuf, y_scr, sem, sem_x, *, final, cap, n_tiles):
    i = pl.program_id(0)
    slot = i % 2
    tm = x_ref.shape[0]
    W = COMBINE_WINDOW
    n_rows = N_EXPERTS * cap

    def win_start(row):
        aligned = jnp.left_shift(jnp.right_shift(row, 4), 4)
        return pl.multiple_of(jnp.minimum(aligned, n_rows - W), BF16_SUBLANES)

    def issue(tile, sl):
        for e in range(N_EXPERTS):
            st = win_start(e * cap + lo_ref[tile * N_EXPERTS + e])
            pltpu.make_async_copy(out_hbm.at[pl.ds(st, W)], wbuf.at[sl, pl.ds(e * W, W)], sem.at[sl]).start()

    @pl.when(i == 0)
    def _():
        issue(0, 0)

    @pl.when(i + 1 < n_tiles)
    def _():
        issue(i + 1, 1 - slot)

    pltpu.make_async_copy(out_hbm.at[pl.ds(0, N_EXPERTS * W)], wbuf.at[slot], sem.at[slot]).wait()

    posv = pos_ref[...]
    lane = lax.broadcasted_iota(jnp.int32, (tm, W), 1).astype(F32)
    parts = []
    for e in range(N_EXPERTS):
        st = win_start(e * cap + lo_ref[i * N_EXPERTS + e])
        rel = posv[:, e:e + 1] - (st - e * cap).astype(F32)
        parts.append((rel == lane).astype(BF16))
    y_scr[...] = jnp.dot(jnp.concatenate(parts, axis=1), wbuf[slot], preferred_element_type=F32)

    n_extra = []
    for e in range(N_EXPERTS):
        first = lo_ref[i * N_EXPERTS + e]
        covered_end = win_start(e * cap + first) - e * cap + W
        remaining = jnp.maximum(first + cnt_ref[i * N_EXPERTS + e] - covered_end, 0)
        n_extra.append((covered_end, (remaining + COMBINE_STRIDE - 1) // COMBINE_STRIDE))

    @pl.when(functools.reduce(lambda a, b: a + b, [n for _, n in n_extra]) > 0)
    def _():
        for e in range(N_EXPERTS):
            covered_end, n_e = n_extra[e]

            def extra(k, carry, e=e, covered_end=covered_end):
                want = covered_end + k * COMBINE_STRIDE
                st = win_start(e * cap + want)
                cp = pltpu.make_async_copy(out_hbm.at[pl.ds(st, W)], xbuf, sem_x)
                cp.start()
                cp.wait()
                pe = pos_ref[:, e:e + 1]
                hit = (pe - (st - e * cap).astype(F32) == lane) & (pe >= want.astype(F32)) \
                    & (pe < (want + COMBINE_STRIDE).astype(F32))
                y_scr[...] += jnp.dot(hit.astype(BF16), xbuf[...], preferred_element_type=F32)
                return carry

            lax.fori_loop(0, n_e, extra, 0)

    x = x_ref[...] + y_scr[...]
    ms = jnp.mean(x * x, axis=-1, keepdims=True)
    h = (x * lax.rsqrt(ms + EPS) * np_ref[...]).astype(BF16)
    gate = _sigmoid(jnp.dot(h, wg_ref[...], preferred_element_type=F32))
    pp = jnp.dot(p_ref[...].astype(BF16), wp_ref[...], preferred_element_type=F32)
    x = x + gate * pp
    if final:
        ms = jnp.mean(x * x, axis=-1, keepdims=True)
        x = x * lax.rsqrt(ms + EPS) * fn_ref[...]
    o_ref[...] = x


def _ple(x2d, out, pos, bex, p2d, norm_ple, wg, wp, final_norm, final):
    T = x2d.shape[0]
    tm = min(TM_PLE, T)
    n_tiles = T // tm
    cap = out.shape[0] // N_EXPERTS
    W = COMBINE_WINDOW
    assert cap >= W and cap % BF16_SUBLANES == 0 and tm % LANES == 0
    lo = bex[:, ::tm // LANES].astype(jnp.int32)
    cnt = jnp.diff(lo, axis=1, append=jnp.full((N_EXPERTS, 1), cap, jnp.int32))
    lo = lo.T.reshape(-1)
    cnt = cnt.T.reshape(-1)
    pos_t = pos.T
    vmem = (2 * (2 * tm * D_MODEL * 4 + tm * LANES * 4 + tm * PLE_DIM * 4 + D_MODEL * D_MODEL * 2
                 + PLE_DIM * D_MODEL * 2) + 2 * N_EXPERTS * W * D_MODEL * 2 + W * D_MODEL * 2
            + tm * N_EXPERTS * W * 2 + 6 * tm * D_MODEL * 4)
    tok = lambda width: pl.BlockSpec((tm, width), lambda i, lo, cnt: (i, 0))
    full = lambda shape: pl.BlockSpec(shape, lambda i, lo, cnt: (0,) * len(shape))
    grid_spec = pltpu.PrefetchScalarGridSpec(
        num_scalar_prefetch=2,
        grid=(n_tiles,),
        in_specs=[tok(D_MODEL), tok(N_EXPERTS), tok(PLE_DIM), full((1, D_MODEL)), full((D_MODEL, D_MODEL)),
                  full((PLE_DIM, D_MODEL)), full((1, D_MODEL)), pl.BlockSpec(memory_space=pl.ANY)],
        out_specs=tok(D_MODEL),
        scratch_shapes=[
            pltpu.VMEM((2, N_EXPERTS * W, D_MODEL), BF16),
            pltpu.VMEM((W, D_MODEL), BF16),
            pltpu.VMEM((tm, D_MODEL), F32),
            pltpu.SemaphoreType.DMA((2,)),
            pltpu.SemaphoreType.DMA,
        ],
    )
    return pl.pallas_call(
        functools.partial(_ple_kernel, final=final, cap=cap, n_tiles=n_tiles),
        grid_spec=grid_spec,
        out_shape=jax.ShapeDtypeStruct((T, D_MODEL), F32),
        compiler_params=pltpu.CompilerParams(
            dimension_semantics=("arbitrary",), vmem_limit_bytes=_vmem_limit(vmem)),
        name="combine_ple",
    )(lo, cnt, x2d, pos_t, p2d, norm_ple, wg, wp, final_norm, out)


def _prep_layer(i, norm_mix, w_in, conv_w, na_rpb, gla_w2, gla_b, gla_norm, w_branch, w_out, norm_ffn, w_router,
                w_exp_gate, w_exp_up, w_exp_down, norm_ple, w_ple_gate, w_ple_proj):
    w = w_in[i]
    sizes = (CONV_WIDTH,) * 3 + (NA_WIDTH,) * 3 + (GLA_KEY_WIDTH,) * 2 + (GLA_VAL_WIDTH,) * 2 + (2 * GLA_RANK,)
    offs = np.concatenate([[0], np.cumsum(sizes)])
    gates_off = int(offs[-1])
    lr_off = int(offs[10])
    w_main = jnp.concatenate([w[:, gates_off:], w[:, :lr_off]], axis=1)
    assert float(np.log2(NA_HEAD_DIM ** -0.5)).is_integer() and float(np.log2(GLA_DK ** -0.5)).is_integer()
    col_scale = np.ones((PROJ_COLS,), np.float32)
    col_scale[COL_NQ:COL_NQ + NA_WIDTH] = NA_HEAD_DIM ** -0.5
    col_scale[COL_GQ:COL_GQ + GLA_KEY_WIDTH] = GLA_DK ** -0.5
    w_main = (w_main * col_scale).astype(BF16)
    w_lr = jnp.zeros((D_MODEL, LANES), F32).at[:, :2 * GLA_RANK].set(w[:, lr_off:gates_off]).astype(BF16)
    return dict(
        norm_mix=norm_mix[i].reshape(1, D_MODEL), w_main=w_main, w_lr=w_lr, conv_w=conv_w[i], na_rpb=na_rpb[i],
        gla_w2=gla_w2[i], gla_b=gla_b[i], gla_norm=gla_norm[i], w_branch=w_branch[i].astype(BF16),
        w_out=w_out[i].astype(BF16), norm_ffn=norm_ffn[i].reshape(1, D_MODEL), w_router=w_router[i],
        w_exp_gate=w_exp_gate[i].astype(BF16), w_exp_up=w_exp_up[i].astype(BF16),
        w_exp_down=w_exp_down[i].astype(BF16), norm_ple=norm_ple[i].reshape(1, D_MODEL),
        w_ple_gate=w_ple_gate[i].astype(BF16), w_ple_proj=w_ple_proj[i].astype(BF16))


def _moe(hp, aff, lp):
    n_tok = hp.shape[0]
    cap = CAPACITY_FACTOR * n_tok // N_EXPERTS
    idx, gate_val, pos, bex = _route(aff, cap)
    out = _moe_ffn(hp, idx, gate_val, lp["w_exp_gate"], lp["w_exp_up"], lp["w_exp_down"])
    return out, pos, bex


def _trunk(x, p, layers, final_norm):
    bsz, L, _ = x.shape
    T = bsz * L
    x2d = x.reshape(T, D_MODEL)
    fn = final_norm.reshape(1, D_MODEL)
    for i, lp in enumerate(layers):
        proj, lr = _inproj(x2d, lp["norm_mix"], lp["w_main"], lp["w_lr"])
        y_b = _na_attention(proj, lp["na_rpb"], bsz, L)
        y_c = _gla(proj, lr, lp["gla_w2"], lp["gla_b"], lp["gla_norm"], bsz, L)
        x2d, h, aff = _merge(x2d, proj, y_b, y_c, lp["conv_w"], lp["w_branch"], lp["w_out"], lp["norm_ffn"],
                             lp["w_router"], L)
        out, pos, bex = _moe(h, aff, lp)
        x2d = _ple(x2d, out, pos, bex, p[i].reshape(T, PLE_DIM), lp["norm_ple"], lp["w_ple_gate"], lp["w_ple_proj"], fn,
                   final=(i == len(layers) - 1))
    return x2d.reshape(bsz, L, D_MODEL)


def kernel(x_prompt, x_sample, p_prompt, p_sample, norm_mix, w_in, conv_w, na_rpb, gla_w2, gla_b, gla_norm, w_branch, w_out, norm_ffn, w_router, w_exp_gate, w_exp_up, w_exp_down, norm_ple, w_ple_gate, w_ple_proj, final_norm):
    depth = w_in.shape[0]
    layers = [_prep_layer(i, norm_mix, w_in, conv_w, na_rpb, gla_w2, gla_b, gla_norm, w_branch, w_out, norm_ffn,
                          w_router, w_exp_gate, w_exp_up, w_exp_down, norm_ple, w_ple_gate, w_ple_proj)
              for i in range(depth)]
    y_prompt = _trunk(x_prompt, p_prompt, layers, final_norm)
    y_sample = _trunk(x_sample, p_sample, layers, final_norm)
    return (y_prompt, y_sample)
```

```python
import functools

import numpy as np
import jax
import jax.numpy as jnp
from jax import lax
from jax.experimental import pallas as pl
from jax.experimental.pallas import tpu as pltpu

F32 = jnp.float32
BF16 = jnp.bfloat16

D_MODEL = 1024
GRID_W = 64
PLE_DIM = 256
EPS = 1e-6
CONV_WIDTH = 512
CONV_K = 3
NA_HEADS = 8
NA_HEAD_DIM = 64
NA_WIDTH = NA_HEADS * NA_HEAD_DIM
WIN_ROWS = 8
WIN_COLS = 16
GLA_HEADS = 4
GLA_DK = 64
GLA_DV = 128
GLA_KEY_WIDTH = GLA_HEADS * GLA_DK
GLA_VAL_WIDTH = GLA_HEADS * GLA_DV
GLA_RANK = 16
GLA_GATE_NORM = 16.0
GLA_CHUNK = 64
N_BRANCH = 3
N_EXPERTS = 16
EXPERT_FF = 2048
CAPACITY_FACTOR = 2

V7X_VMEM_BYTES = 64 * 1024 * 1024
LANES = 128
BF16_SUBLANES = 16

PROJ_TN = 512
COL_GATES = 0
COL_AB = 3 * D_MODEL
COL_AC = COL_AB + CONV_WIDTH
COL_AX = COL_AC + CONV_WIDTH
COL_NQ = COL_AX + CONV_WIDTH
COL_NK = COL_NQ + NA_WIDTH
COL_NV = COL_NK + NA_WIDTH
COL_GQ = COL_NV + NA_WIDTH
COL_GK = COL_GQ + GLA_KEY_WIDTH
COL_GV = COL_GK + GLA_KEY_WIDTH
COL_GG = COL_GV + GLA_VAL_WIDTH
PROJ_COLS = COL_GG + GLA_VAL_WIDTH
N_GATE_BLOCKS = (3 * D_MODEL) // PROJ_TN

TM_INPROJ = 512
NA_TILE_ROWS = 4
NA_TQ = NA_TILE_ROWS * GRID_W
GLA_TC = 512
TM_MERGE = 256
TM_FFN = 512
TM_PLE = 512

NEG_INF = -1e30


def _vmem_limit(nbytes):
    return int(min(V7X_VMEM_BYTES - 6 * 1024 * 1024, max(32 * 1024 * 1024, nbytes + 8 * 1024 * 1024)))


def _sigmoid(x):
    return 1.0 / (1.0 + jnp.exp(-x))


def _split3(a):
    hi = a.astype(BF16)
    r1 = a - hi.astype(F32)
    mid = r1.astype(BF16)
    lo = (r1 - mid.astype(F32)).astype(BF16)
    return hi, mid, lo


def _inproj_kernel(x_ref, nw_ref, wlr_ref, w_hbm, proj_ref, lr_ref, w_vmem, sem):
    @pl.when(pl.program_id(0) == 0)
    def _():
        cp = pltpu.make_async_copy(w_hbm, w_vmem, sem)
        cp.start()
        cp.wait()

    x = x_ref[...]
    ms = jnp.mean(x * x, axis=-1, keepdims=True)
    hb = (x * lax.rsqrt(ms + EPS) * nw_ref[...]).astype(BF16)
    lr_ref[...] = jnp.dot(hb, wlr_ref[...], preferred_element_type=F32)
    for j in range(PROJ_COLS // PROJ_TN):
        cs = slice(j * PROJ_TN, (j + 1) * PROJ_TN)
        acc = jnp.dot(hb, w_vmem[:, cs], preferred_element_type=F32)
        if j < N_GATE_BLOCKS:
            acc = _sigmoid(acc)
        proj_ref[:, cs] = acc.astype(BF16)


def _inproj(x2d, norm_w, w_main, w_lr):
    T = x2d.shape[0]
    tm = min(TM_INPROJ, T)
    vmem = (2 * (tm * D_MODEL * 4 + D_MODEL * LANES * 2 + tm * PROJ_COLS * 2 + tm * LANES * 4)
            + D_MODEL * PROJ_COLS * 2 + tm * D_MODEL * 2 + 4 * tm * PROJ_TN * 4)
    return pl.pallas_call(
        _inproj_kernel,
        grid=(T // tm,),
        in_specs=[
            pl.BlockSpec((tm, D_MODEL), lambda i: (i, 0)),
            pl.BlockSpec((1, D_MODEL), lambda i: (0, 0)),
            pl.BlockSpec((D_MODEL, LANES), lambda i: (0, 0)),
            pl.BlockSpec(memory_space=pl.ANY),
        ],
        out_specs=[
            pl.BlockSpec((tm, PROJ_COLS), lambda i: (i, 0)),
            pl.BlockSpec((tm, LANES), lambda i: (i, 0)),
        ],
        out_shape=[
            jax.ShapeDtypeStruct((T, PROJ_COLS), BF16),
            jax.ShapeDtypeStruct((T, LANES), F32),
        ],
        scratch_shapes=[pltpu.VMEM((D_MODEL, PROJ_COLS), BF16), pltpu.SemaphoreType.DMA],
        compiler_params=pltpu.CompilerParams(
            dimension_semantics=("arbitrary",), vmem_limit_bytes=_vmem_limit(vmem)),
        name="inproj",
    )(x2d, norm_w, w_lr, w_main)


def _na_tables(rows):
    R = NA_TILE_ROWS
    J = rows // R
    kr = min(WIN_ROWS, rows)
    qr_rel = np.arange(NA_TQ) // GRID_W
    qc = np.arange(NA_TQ) % GRID_W
    kk = np.arange(3 * NA_TQ)
    kblk = kk // NA_TQ
    kwithin = kk % NA_TQ
    kc = kwithin % GRID_W
    col_start = np.clip(qc - WIN_COLS // 2, 0, GRID_W - WIN_COLS)
    col_ok = (kc[None, :] >= col_start[:, None]) & (kc[None, :] < col_start[:, None] + WIN_COLS)
    dc = np.clip(kc[None, :] - qc[:, None] + WIN_COLS - 1, 0, 2 * WIN_COLS - 2)
    types, keys, tile_type = [], {}, []
    for j in range(J):
        blocks = np.array([max(j - 1, 0), j, min(j + 1, J - 1)])
        blk_ok = np.array([j - 1 >= 0, True, j + 1 <= J - 1])
        krow = blocks[kblk] * R + kwithin // GRID_W
        qrow = j * R + qr_rel
        rs = np.clip(qrow - kr // 2, 0, rows - kr)
        row_ok = (krow[None, :] >= rs[:, None]) & (krow[None, :] < rs[:, None] + kr) & blk_ok[kblk][None, :]
        mask = row_ok & col_ok
        dr = np.clip(krow[None, :] - qrow[:, None] + WIN_ROWS - 1, 0, 2 * WIN_ROWS - 2)
        dr = np.where(mask, dr, 0)
        key = mask.tobytes() + dr.astype(np.int8).tobytes()
        if key not in keys:
            keys[key] = len(types)
            types.append((mask, dr, np.where(mask, dc, 0)))
        tile_type.append(keys[key])
    mask = np.stack([t[0] for t in types])
    dr = np.stack([t[1] for t in types]).astype(np.int32)
    dcs = np.stack([t[2] for t in types]).astype(np.int32)
    return mask, dr, dcs, np.array(tile_type, np.int32)


def _na_bias(rpb, mask, dr):
    n_types = mask.shape[0]
    W = GRID_W
    qc = np.arange(W)[:, None]
    kc = np.arange(W)[None, :]
    col_start = np.clip(qc - WIN_COLS // 2, 0, W - WIN_COLS)
    col_ok = (kc >= col_start) & (kc < col_start + WIN_COLS)
    n_dc = 2 * WIN_COLS - 1
    lead = W - WIN_COLS
    padded = jnp.pad(rpb, ((0, 0), (0, 0), (lead, 2 * W - 1 - lead - n_dc)), constant_values=NEG_INF)
    toep = jnp.stack([padded[:, :, W - 1 - q:2 * W - 1 - q] for q in range(W)], axis=2)
    toep = jnp.where(col_ok[None, None], toep, NEG_INF)
    dead = jnp.full((NA_HEADS, 1, W, W), NEG_INF, F32)
    blocks = jnp.concatenate([toep, dead], axis=1)
    n_dead = toep.shape[1]
    tables = []
    for t in range(n_types):
        ids = np.where(mask[t, ::W, ::W], dr[t, ::W, ::W], n_dead)
        rows_ = [jnp.concatenate([blocks[:, int(b)] for b in ids_row], axis=-1) for ids_row in ids]
        tables.append(jnp.concatenate(rows_, axis=1))
    return jnp.stack(tables)


def _na_kernel(tid_ref, q_ref, kp_ref, kc_ref, kn_ref, vp_ref, vc_ref, vn_ref, bias_ref, o_ref):
    del tid_ref
    q = q_ref[...]
    k = jnp.concatenate([kp_ref[...], kc_ref[...], kn_ref[...]], axis=0)
    v = jnp.concatenate([vp_ref[...], vc_ref[...], vn_ref[...]], axis=0)
    outs = []
    for h in range(NA_HEADS):
        sl = slice(h * NA_HEAD_DIM, (h + 1) * NA_HEAD_DIM)
        s = lax.dot_general(q[:, sl], k[:, sl], (((1,), (1,)), ((), ())), preferred_element_type=F32)
        s = s + bias_ref[0, h]
        m = jnp.max(s, axis=-1, keepdims=True)
        p = jnp.exp(s - m)
        l = jnp.sum(p, axis=-1, keepdims=True)
        o = jnp.dot(p.astype(BF16), v[:, sl], preferred_element_type=F32)
        outs.append(o * (1.0 / l))
    o_ref[...] = jnp.concatenate(outs, axis=-1).astype(BF16)


def _na_attention(proj, rpb, bsz, L):
    T = proj.shape[0]
    rows = L // GRID_W
    J = rows // NA_TILE_ROWS
    mask, dr, _, tile_type = _na_tables(rows)
    bias = _na_bias(rpb.astype(F32), mask, dr)
    qb, kb, vb = COL_NQ // NA_WIDTH, COL_NK // NA_WIDTH, COL_NV // NA_WIDTH

    def prev(b, j, tid):
        return b * J + jnp.maximum(j - 1, 0)

    def nxt(b, j, tid):
        return b * J + jnp.minimum(j + 1, J - 1)

    blk = (NA_TQ, NA_WIDTH)
    vmem = 2 * (8 * NA_TQ * NA_WIDTH * 2 + NA_HEADS * NA_TQ * 3 * NA_TQ * 4) + 8 * NA_TQ * 3 * NA_TQ * 4
    grid_spec = pltpu.PrefetchScalarGridSpec(
        num_scalar_prefetch=1,
        grid=(bsz, J),
        in_specs=[
            pl.BlockSpec(blk, lambda b, j, tid: (b * J + j, qb)),
            pl.BlockSpec(blk, lambda b, j, tid: (prev(b, j, tid), kb)),
            pl.BlockSpec(blk, lambda b, j, tid: (b * J + j, kb)),
            pl.BlockSpec(blk, lambda b, j, tid: (nxt(b, j, tid), kb)),
            pl.BlockSpec(blk, lambda b, j, tid: (prev(b, j, tid), vb)),
            pl.BlockSpec(blk, lambda b, j, tid: (b * J + j, vb)),
            pl.BlockSpec(blk, lambda b, j, tid: (nxt(b, j, tid), vb)),
            pl.BlockSpec((1, NA_HEADS, NA_TQ, 3 * NA_TQ), lambda b, j, tid: (tid[j], 0, 0, 0)),
        ],
        out_specs=pl.BlockSpec(blk, lambda b, j, tid: (b * J + j, 0)),
    )
    return pl.pallas_call(
        _na_kernel,
        grid_spec=grid_spec,
        out_shape=jax.ShapeDtypeStruct((T, NA_WIDTH), BF16),
        compiler_params=pltpu.CompilerParams(
            dimension_semantics=("arbitrary", "arbitrary"), vmem_limit_bytes=_vmem_limit(vmem)),
        name="na_attention",
    )(jnp.asarray(tile_type), proj, proj, proj, proj, proj, proj, proj, bias)


def _gla_kernel(q_ref, k_ref, v_ref, gg_ref, lr_ref, cum_ref, w2_ref, gb_ref, gn_ref, o_ref, ofwd_scr, st_scr,
                *, nt):
    p = pl.program_id(1)
    j = pl.program_id(2)
    c = GLA_CHUNK
    n_chunks = GLA_TC // c
    H = GLA_HEADS

    @pl.when(j == 0)
    def _():
        st_scr[...] = jnp.zeros_like(st_scr)

    lane_head = lax.broadcasted_iota(jnp.int32, (1, GLA_KEY_WIDTH), 1) // GLA_DK
    ri = lax.broadcasted_iota(jnp.int32, (H * c, c), 0) % c
    ci = lax.broadcasted_iota(jnp.int32, (H * c, c), 1)

    def sweep(direction):
        tri = (ci <= ri) if direction == 0 else (ci >= ri)
        z = lr_ref[...].astype(BF16)
        xg = jnp.dot(z, w2_ref[direction], preferred_element_type=F32) + gb_ref[direction]
        la = (jnp.minimum(xg, 0.0) - jnp.log1p(jnp.exp(-jnp.abs(xg)))) * (1.0 / GLA_GATE_NORM)
        hi, mid, lo = _split3(la)
        cum = cum_ref[direction]
        b_all = (jnp.dot(cum, hi, preferred_element_type=F32)
                 + jnp.dot(cum, mid, preferred_element_type=F32)
                 + jnp.dot(cum, lo, preferred_element_type=F32))
        k_all = k_ref[...].astype(F32)
        qd_all = q_ref[...].astype(F32) * jnp.exp(b_all)
        kd_all = (k_all * jnp.exp(-b_all)).astype(BF16)
        st = st_scr[...]
        order = range(n_chunks) if direction == 0 else range(n_chunks - 1, -1, -1)
        for ch in order:
            rs = slice(ch * c, (ch + 1) * c)
            b = b_all[rs]
            b_last = b[c - 1:c, :] if direction == 0 else b[0:1, :]
            v = v_ref[rs, :]
            qd = qd_all[rs]
            kd = kd_all[rs]
            krem = (k_all[rs] * jnp.exp(b_last - b)).astype(BF16)
            st_b = st.astype(BF16)
            nt_dims = (((1,), (1,)), ((), ()))
            qs = jnp.concatenate([jnp.where(lane_head == h, qd, 0.0) for h in range(H)], axis=0).astype(BF16)
            att = lax.dot_general(qs, kd, nt_dims, preferred_element_type=F32)
            att = jnp.where(tri, att, 0.0).astype(BF16)
            intra = jnp.dot(att, v, preferred_element_type=F32)
            inter = lax.dot_general(qs, st_b, nt_dims, preferred_element_type=F32)
            o_chunk = jnp.concatenate(
                [intra[h * c:(h + 1) * c, h * GLA_DV:(h + 1) * GLA_DV] + inter[h * c:(h + 1) * c]
                 for h in range(H)], axis=-1)
            kv = lax.dot_general(v, krem, (((0,), (0,)), ((), ())), preferred_element_type=F32)
            st = st * jnp.exp(b_last)
            for h in range(H):
                st = st + jnp.where(lane_head == h, kv[h * GLA_DV:(h + 1) * GLA_DV, :], 0.0)
            yield ch, o_chunk
        st_scr[...] = st

    @pl.when(p == 0)
    def _():
        for ch, o_chunk in sweep(0):
            start = pl.multiple_of(j * GLA_TC + ch * c, c)
            ofwd_scr[pl.ds(start, c), :] = o_chunk

    @pl.when(p == 1)
    def _():
        t = nt - 1 - j
        gn = gn_ref[...]
        for ch, o_chunk in sweep(1):
            start = pl.multiple_of(t * GLA_TC + ch * c, c)
            o = ofwd_scr[pl.ds(start, c), :] + o_chunk
            normed = []
            for h in range(H):
                oh = o[:, h * GLA_DV:(h + 1) * GLA_DV]
                ms = jnp.mean(oh * oh, axis=-1, keepdims=True)
                normed.append(oh * lax.rsqrt(ms + EPS))
            on = jnp.concatenate(normed, axis=-1) * gn
            g = gg_ref[ch * c:(ch + 1) * c, :].astype(F32)
            o_ref[ch * c:(ch + 1) * c, :] = (g * _sigmoid(g) * on).astype(BF16)


def _gla(proj, lr, gla_w2, gla_b, gla_norm, bsz, L):
    T = proj.shape[0]
    nt = L // GLA_TC
    w2p = jnp.zeros((2, LANES, GLA_KEY_WIDTH), F32)
    w2p = w2p.at[0, 0:GLA_RANK].set(gla_w2[0]).at[1, GLA_RANK:2 * GLA_RANK].set(gla_w2[1]).astype(BF16)
    gb = gla_b.reshape(2, 1, GLA_KEY_WIDTH).astype(F32)
    gn = gla_norm.reshape(1, GLA_VAL_WIDTH).astype(F32)
    r = np.arange(GLA_TC)
    same = (r[:, None] // GLA_CHUNK) == (r[None, :] // GLA_CHUNK)
    cum = jnp.asarray(np.stack([same & (r[None, :] <= r[:, None]), same & (r[None, :] >= r[:, None])]), BF16)

    def tile(b, p, j):
        return b * nt + jnp.where(p == 0, j, nt - 1 - j)

    def otile(b, p, j):
        return b * nt + jnp.where(p == 0, nt - 1, nt - 1 - j)

    kq, kk = COL_GQ // GLA_KEY_WIDTH, COL_GK // GLA_KEY_WIDTH
    kv, kg = COL_GV // GLA_VAL_WIDTH, COL_GG // GLA_VAL_WIDTH
    vmem = (2 * (2 * GLA_TC * GLA_KEY_WIDTH * 2 + 3 * GLA_TC * GLA_VAL_WIDTH * 2 + GLA_TC * LANES * 4
                 + 2 * GLA_TC * GLA_TC * 2)
            + L * GLA_VAL_WIDTH * 4 + GLA_DV * GLA_KEY_WIDTH * 4 + 16 * GLA_TC * GLA_KEY_WIDTH * 4)
    return pl.pallas_call(
        functools.partial(_gla_kernel, nt=nt),
        grid=(bsz, 2, nt),
        in_specs=[
            pl.BlockSpec((GLA_TC, GLA_KEY_WIDTH), lambda b, p, j: (tile(b, p, j), kq)),
            pl.BlockSpec((GLA_TC, GLA_KEY_WIDTH), lambda b, p, j: (tile(b, p, j), kk)),
            pl.BlockSpec((GLA_TC, GLA_VAL_WIDTH), lambda b, p, j: (tile(b, p, j), kv)),
            pl.BlockSpec((GLA_TC, GLA_VAL_WIDTH), lambda b, p, j: (tile(b, p, j), kg)),
            pl.BlockSpec((GLA_TC, LANES), lambda b, p, j: (tile(b, p, j), 0)),
            pl.BlockSpec((2, GLA_TC, GLA_TC), lambda b, p, j: (0, 0, 0)),
            pl.BlockSpec((2, LANES, GLA_KEY_WIDTH), lambda b, p, j: (0, 0, 0)),
            pl.BlockSpec((2, 1, GLA_KEY_WIDTH), lambda b, p, j: (0, 0, 0)),
            pl.BlockSpec((1, GLA_VAL_WIDTH), lambda b, p, j: (0, 0)),
        ],
        out_specs=pl.BlockSpec((GLA_TC, GLA_VAL_WIDTH), lambda b, p, j: (otile(b, p, j), 0)),
        out_shape=jax.ShapeDtypeStruct((T, GLA_VAL_WIDTH), BF16),
        scratch_shapes=[pltpu.VMEM((L, GLA_VAL_WIDTH), F32), pltpu.VMEM((GLA_DV, GLA_KEY_WIDTH), F32)],
        compiler_params=pltpu.CompilerParams(
            dimension_semantics=("arbitrary", "arbitrary", "arbitrary"), vmem_limit_bytes=_vmem_limit(vmem)),
        name="gla",
    )(proj, proj, proj, proj, lr, cum, w2p, gb, gn)


def _merge_kernel(x_ref, ga_ref, gb_ref, gc_ref, ab_ref, ac_ref, ax_ref, acp_ref, axp_ref, acn_ref, axn_ref,
                  yb_ref, yc_ref, cw_ref, wb_ref, wo_ref, nf_ref, wr_ref, xo_ref, h_ref, aff_ref, *, tiles_per_seq):
    i = pl.program_id(0)
    tm = x_ref.shape[0]
    pos = i % tiles_per_seq
    has_prev = jnp.where(pos == 0, 0.0, 1.0)
    has_next = jnp.where(pos == tiles_per_seq - 1, 0.0, 1.0)
    u = ac_ref[...].astype(F32) * ax_ref[...].astype(F32)
    up = (acp_ref[...].astype(F32) * axp_ref[...].astype(F32))[BF16_SUBLANES - 1:BF16_SUBLANES, :] * has_prev
    un = (acn_ref[...].astype(F32) * axn_ref[...].astype(F32))[0:1, :] * has_next
    ridx = lax.broadcasted_iota(jnp.int32, (tm, 1), 0)
    u_prev = jnp.where(ridx == 0, up, pltpu.roll(u, 1, axis=0))
    u_next = jnp.where(ridx == tm - 1, un, pltpu.roll(u, tm - 1, axis=0))
    cw = cw_ref[...]
    y_a = ab_ref[...].astype(F32) * (u_prev * cw[0:1, :] + u * cw[1:2, :] + u_next * cw[2:3, :])
    m = ga_ref[...].astype(F32) * jnp.dot(y_a.astype(BF16), wb_ref[0], preferred_element_type=F32)
    m = m + gb_ref[...].astype(F32) * jnp.dot(yb_ref[...], wb_ref[1], preferred_element_type=F32)
    m = m + gc_ref[...].astype(F32) * jnp.dot(yc_ref[...], wb_ref[2], preferred_element_type=F32)
    x_new = x_ref[...] + jnp.dot(m.astype(BF16), wo_ref[...], preferred_element_type=F32)
    xo_ref[...] = x_new
    ms = jnp.mean(x_new * x_new, axis=-1, keepdims=True)
    h = x_new * lax.rsqrt(ms + EPS) * nf_ref[...]
    bits = lax.bitcast_convert_type(h.astype(BF16).astype(F32), jnp.uint32)
    half = D_MODEL // 2
    h_ref[...] = (bits[:, :half] >> 16) | (bits[:, half:] & jnp.uint32(0xFFFF0000))
    h0, h1, _ = _split3(h)
    w0, w1 = wr_ref[0], wr_ref[1]
    logits = (jnp.dot(h0, w0, preferred_element_type=F32) + jnp.dot(h0, w1, preferred_element_type=F32)
              + jnp.dot(h1, w0, preferred_element_type=F32))
    lane = lax.broadcasted_iota(jnp.int32, logits.shape, 1)
    logits = jnp.where(lane < N_EXPERTS, logits, NEG_INF)
    mx = jnp.max(logits, axis=-1, keepdims=True)
    e = jnp.exp(logits - mx)
    aff_ref[...] = e / jnp.sum(e, axis=-1, keepdims=True)


def _merge(x2d, proj, y_b, y_c, conv_w, wb, wo, norm_ffn, w_router, L):
    T = x2d.shape[0]
    tm = min(TM_MERGE, L)
    tiles_per_seq = L // tm
    n16 = T // BF16_SUBLANES
    per16 = tm // BF16_SUBLANES
    wr = jnp.zeros((D_MODEL, LANES), F32).at[:, :N_EXPERTS].set(w_router)
    wr = jnp.stack(_split3(wr)[:2])
    cb = CONV_WIDTH
    vmem = 2 * (2 * tm * D_MODEL * 4 + 3 * tm * D_MODEL * 2 + 5 * tm * cb * 2 + 4 * BF16_SUBLANES * cb * 2
                + 3 * cb * D_MODEL * 2 + D_MODEL * D_MODEL * 2 + 3 * D_MODEL * LANES * 2
                + tm * D_MODEL * 2 + tm * LANES * 4) + 8 * tm * D_MODEL * 4
    tok = lambda width, col: pl.BlockSpec((tm, width), lambda i: (i, col // width))
    halo_p = lambda col: pl.BlockSpec((BF16_SUBLANES, cb), lambda i: (jnp.maximum(i * per16 - 1, 0), col // cb))
    halo_n = lambda col: pl.BlockSpec((BF16_SUBLANES, cb),
                                      lambda i: (jnp.minimum((i + 1) * per16, n16 - 1), col // cb))
    full = lambda shape: pl.BlockSpec(shape, lambda i: (0,) * len(shape))
    return pl.pallas_call(
        functools.partial(_merge_kernel, tiles_per_seq=tiles_per_seq),
        grid=(T // tm,),
        in_specs=[
            tok(D_MODEL, 0),
            tok(D_MODEL, COL_GATES), tok(D_MODEL, COL_GATES + D_MODEL), tok(D_MODEL, COL_GATES + 2 * D_MODEL),
            tok(cb, COL_AB), tok(cb, COL_AC), tok(cb, COL_AX),
            halo_p(COL_AC), halo_p(COL_AX), halo_n(COL_AC), halo_n(COL_AX),
            tok(NA_WIDTH, 0), tok(GLA_VAL_WIDTH, 0),
            full((CONV_K, cb)), full((N_BRANCH, cb, D_MODEL)), full((D_MODEL, D_MODEL)),
            full((1, D_MODEL)), full((2, D_MODEL, LANES)),
        ],
        out_specs=[tok(D_MODEL, 0), tok(D_MODEL // 2, 0), tok(LANES, 0)],
        out_shape=[
            jax.ShapeDtypeStruct((T, D_MODEL), F32),
            jax.ShapeDtypeStruct((T, D_MODEL // 2), jnp.uint32),
            jax.ShapeDtypeStruct((T, LANES), F32),
        ],
        compiler_params=pltpu.CompilerParams(
            dimension_semantics=("arbitrary",), vmem_limit_bytes=_vmem_limit(vmem)),
        name="merge",
    )(x2d, proj, proj, proj, proj, proj, proj, proj, proj, proj, proj, y_b, y_c,
      conv_w, wb, wo, norm_ffn, wr)


DMA_UNROLL = 8


def _moe_kernel(idx_ref, idxn_ref, g_ref, wg_ref, wu_ref, wd_ref, hp_hbm, o_ref, hbuf, sem_h, *, n_steps):
    tm = hbuf.shape[1]
    s = pl.program_id(0)
    slot = s % 2

    def gather_rows(idx, sl):
        def body(g, c):
            for k in range(DMA_UNROLL):
                r = g * DMA_UNROLL + k
                t = idx[0, 0, r]
                pltpu.make_async_copy(hp_hbm.at[pl.ds(t, 1)], hbuf.at[sl, pl.ds(r, 1)],
                                      sem_h.at[sl]).start(priority=k % 2)
            return c
        lax.fori_loop(0, tm // DMA_UNROLL, body, 0)

    @pl.when(s == 0)
    def _():
        gather_rows(idx_ref, slot)

    @pl.when(s + 1 < n_steps)
    def _():
        gather_rows(idxn_ref, 1 - slot)

    pltpu.make_async_copy(hp_hbm.at[pl.ds(0, tm)], hbuf.at[slot], sem_h.at[slot]).wait()
    w = hbuf[slot]
    lo = lax.bitcast_convert_type(w << 16, F32)
    hi = lax.bitcast_convert_type(w & jnp.uint32(0xFFFF0000), F32)
    xe = jnp.concatenate([lo, hi], axis=1).astype(BF16)
    half = EXPERT_FF // 2
    acc = None
    for f in range(2):
        fs = slice(f * half, (f + 1) * half)
        a = jnp.dot(xe, wg_ref[0, :, fs], preferred_element_type=F32)
        b = jnp.dot(xe, wu_ref[0, :, fs], preferred_element_type=F32)
        he = (a * _sigmoid(a) * b).astype(BF16)
        part = jnp.dot(he, wd_ref[0, fs, :], preferred_element_type=F32)
        acc = part if acc is None else acc + part
    o_ref[...] = (acc * g_ref[0]).astype(BF16)


def _moe_ffn(hp, idx, gate, wg, wu, wd):
    E, cap = idx.shape
    tm = min(TM_FFN, cap)
    assert tm % DMA_UNROLL == 0
    nt = cap // tm
    n_steps = E * nt
    half = D_MODEL // 2
    idx3 = idx.reshape(n_steps, 1, tm).astype(jnp.int32)
    g3 = gate.reshape(n_steps, tm, 1).astype(F32)
    vmem = (2 * (3 * D_MODEL * EXPERT_FF * 2 + tm * LANES * 4 + tm * D_MODEL * 2) + 2 * tm * half * 4
            + 3 * tm * EXPERT_FF * 4)
    smem = lambda f: pl.BlockSpec((1, 1, tm), f, memory_space=pltpu.SMEM)
    wspec = lambda shape: pl.BlockSpec((1,) + shape, lambda s: (s // nt, 0, 0))
    return pl.pallas_call(
        functools.partial(_moe_kernel, n_steps=n_steps),
        grid=(n_steps,),
        in_specs=[
            smem(lambda s: (s, 0, 0)),
            smem(lambda s: (jnp.minimum(s + 1, n_steps - 1), 0, 0)),
            pl.BlockSpec((1, tm, 1), lambda s: (s, 0, 0)),
            wspec((D_MODEL, EXPERT_FF)), wspec((D_MODEL, EXPERT_FF)), wspec((EXPERT_FF, D_MODEL)),
            pl.BlockSpec(memory_space=pl.ANY),
        ],
        out_specs=pl.BlockSpec((tm, D_MODEL), lambda s: (s, 0)),
        out_shape=jax.ShapeDtypeStruct((E * cap, D_MODEL), BF16),
        scratch_shapes=[pltpu.VMEM((2, tm, half), jnp.uint32), pltpu.SemaphoreType.DMA((2,))],
        compiler_params=pltpu.CompilerParams(
            dimension_semantics=("arbitrary",), vmem_limit_bytes=_vmem_limit(vmem)),
        name="moe_ffn",
    )(idx3, idx3, g3, wg, wu, wd, hp)


ROUTE_SLOT_CHUNK = 1024


def _select_kernel(a_ref, sel_ref, *, cap):
    keys = lax.bitcast_convert_type(a_ref[...], jnp.int32)
    n_exp, n_tok = keys.shape
    capf = jnp.float32(cap)

    def count(pred):
        return jnp.sum(pred.astype(F32), axis=1, keepdims=True)

    def value_bit(b, lo):
        cand = lo | jnp.left_shift(jnp.int32(1), 30 - b)
        return jnp.where(count(keys >= cand) >= capf, cand, lo)

    thr = lax.fori_loop(0, 31, value_bit, jnp.zeros((n_exp, 1), jnp.int32))
    gt = keys > thr
    eq = keys == thr
    need = capf - count(gt)
    tok = lax.broadcasted_iota(jnp.int32, keys.shape, 1)
    nbits = int(n_tok).bit_length()

    def index_bit(b, m):
        cand = m | jnp.left_shift(jnp.int32(1), nbits - 1 - b)
        ok = (cand <= n_tok) & (count(eq & (tok < cand)) <= need)
        return jnp.where(ok, cand, m)

    m = lax.fori_loop(0, nbits, index_bit, jnp.zeros((n_exp, 1), jnp.int32))
    sel_ref[...] = (gt | (eq & (tok < m))).astype(F32)


def _compact_kernel(sel_ref, aff_ref, idx_ref, gate_ref, pos_ref, bex_ref, *, cap, n_tok):
    sel = sel_ref[0]
    nb = sel.shape[0]
    ch = min(ROUTE_SLOT_CHUNK, cap)
    nt_dims = (((1,), (1,)), ((), ()))
    r128 = lax.broadcasted_iota(jnp.int32, (LANES, LANES), 0)
    c128 = lax.broadcasted_iota(jnp.int32, (LANES, LANES), 1)
    rj = lax.broadcasted_iota(jnp.int32, (nb, nb), 0)
    cj = lax.broadcasted_iota(jnp.int32, (nb, nb), 1)
    selb = sel.astype(BF16)
    cin = jnp.dot(selb, (r128 <= c128).astype(BF16), preferred_element_type=F32)
    totb = jnp.broadcast_to(cin[:, LANES - 1:LANES], (nb, LANES)).astype(BF16)
    bex = jnp.dot((cj < rj).astype(BF16), totb, preferred_element_type=F32)
    tot_row = lax.dot_general(jnp.ones((8, LANES), BF16), selb, nt_dims, preferred_element_type=F32)
    binc_row = jnp.dot(tot_row.astype(BF16), (rj <= cj).astype(BF16), preferred_element_type=F32)[0:1]
    pos_ref[0] = jnp.where(sel > 0.5, bex + cin - 1.0, -1.0)
    bex_ref[0] = bex[:, 0:1]
    bex_hi = jnp.floor(bex * (1.0 / 256.0))
    bex_lo = bex - 256.0 * bex_hi
    a_hi, a_mid, a_lo = _split3(aff_ref[0])
    table = jnp.concatenate([cin.astype(BF16), bex_hi.astype(BF16), bex_lo.astype(BF16), a_hi, a_mid, a_lo], axis=1)
    lane_nb = lax.broadcasted_iota(jnp.int32, (ch, nb), 1).astype(F32)
    lane128 = lax.broadcasted_iota(jnp.int32, (ch, LANES), 1).astype(F32)
    for c in range(cap // ch):
        s = (lax.broadcasted_iota(jnp.int32, (ch, 1), 0) + c * ch).astype(F32)
        blk = jnp.sum((binc_row <= s).astype(F32), axis=1, keepdims=True)
        onehot = (lane_nb == blk).astype(BF16)
        g = jnp.dot(onehot, table, preferred_element_type=F32)
        cinrow = g[:, 0:LANES]
        before = g[:, LANES:2 * LANES] * 256.0 + g[:, 2 * LANES:3 * LANES]
        affrow = g[:, 3 * LANES:4 * LANES] + g[:, 4 * LANES:5 * LANES] + g[:, 5 * LANES:6 * LANES]
        off = jnp.sum((cinrow <= s - before).astype(F32), axis=1, keepdims=True)
        idx = jnp.minimum(blk * float(LANES) + off, float(n_tok - 1))
        gate = jnp.sum(jnp.where(lane128 == off, affrow, 0.0), axis=1, keepdims=True)
        idx_t = jnp.transpose(jnp.broadcast_to(idx, (ch, LANES)))[0:1]
        gate_t = jnp.transpose(jnp.broadcast_to(gate, (ch, LANES)))[0:1]
        idx_ref[0, :, c * ch:(c + 1) * ch] = idx_t.astype(jnp.int32)
        gate_ref[0, :, c * ch:(c + 1) * ch] = gate_t


def _route(aff, cap):
    n_tok = aff.shape[0]
    nb = n_tok // LANES
    a_t = aff[:, :N_EXPERTS].T
    sel = pl.pallas_call(
        functools.partial(_select_kernel, cap=cap),
        out_shape=jax.ShapeDtypeStruct((N_EXPERTS, n_tok), F32),
        compiler_params=pltpu.CompilerParams(vmem_limit_bytes=_vmem_limit(12 * N_EXPERTS * n_tok * 4)),
        name="route_select",
    )(a_t)
    ch = min(ROUTE_SLOT_CHUNK, cap)
    vmem = 2 * (2 * nb * LANES * 4 + 2 * cap * 4) + 2 * nb * nb * 4 + nb * 6 * LANES * 4 + ch * (nb + 12 * LANES) * 4
    blk = pl.BlockSpec((1, nb, LANES), lambda e: (e, 0, 0))
    oblk = pl.BlockSpec((1, 1, cap), lambda e: (e, 0, 0))
    idx, gate, pos, bex = pl.pallas_call(
        functools.partial(_compact_kernel, cap=cap, n_tok=n_tok),
        grid=(N_EXPERTS,),
        in_specs=[blk, blk],
        out_specs=[oblk, oblk, blk, pl.BlockSpec((1, nb, 1), lambda e: (e, 0, 0))],
        out_shape=[jax.ShapeDtypeStruct((N_EXPERTS, 1, cap), jnp.int32),
                   jax.ShapeDtypeStruct((N_EXPERTS, 1, cap), F32),
                   jax.ShapeDtypeStruct((N_EXPERTS, nb, LANES), F32),
                   jax.ShapeDtypeStruct((N_EXPERTS, nb, 1), F32)],
        compiler_params=pltpu.CompilerParams(
            dimension_semantics=("arbitrary",), vmem_limit_bytes=_vmem_limit(vmem)),
        name="route_compact",
    )(sel.reshape(N_EXPERTS, nb, LANES), a_t.reshape(N_EXPERTS, nb, LANES))
    return (idx.reshape(N_EXPERTS, cap), gate.reshape(N_EXPERTS, cap), pos.reshape(N_EXPERTS, n_tok),
            bex.reshape(N_EXPERTS, nb))


COMBINE_WINDOW = 128
COMBINE_STRIDE = COMBINE_WINDOW - BF16_SUBLANES


def _ple_kernel(lo_ref, cnt_ref, x_ref, pos_ref, p_ref, np_ref, wg_ref, wp_ref, fn_ref, out_hbm, o_ref,
                wbuf, xbuf, y_scr, sem, sem_x, *, final, cap, n_tiles):
    i = pl.program_id(0)
    slot = i % 2
    tm = x_ref.shape[0]
    W = COMBINE_WINDOW
    n_rows = N_EXPERTS * cap

    def win_start(row):
        aligned = jnp.left_shift(jnp.right_shift(row, 4), 4)
        return pl.multiple_of(jnp.minimum(aligned, n_rows - W), BF16_SUBLANES)

    def issue(tile, sl):
        for e in range(N_EXPERTS):
            st = win_start(e * cap + lo_ref[tile * N_EXPERTS + e])
            pltpu.make_async_copy(out_hbm.at[pl.ds(st, W)], wbuf.at[sl, pl.ds(e * W, W)], sem.at[sl]).start()

    @pl.when(i == 0)
    def _():
        issue(0, 0)

    @pl.when(i + 1 < n_tiles)
    def _():
        issue(i + 1, 1 - slot)

    pltpu.make_async_copy(out_hbm.at[pl.ds(0, N_EXPERTS * W)], wbuf.at[slot], sem.at[slot]).wait()

    posv = pos_ref[...]
    lane = lax.broadcasted_iota(jnp.int32, (tm, W), 1).astype(F32)
    parts = []
    for e in range(N_EXPERTS):
        st = win_start(e * cap + lo_ref[i * N_EXPERTS + e])
        rel = posv[:, e:e + 1] - (st - e * cap).astype(F32)
        parts.append((rel == lane).astype(BF16))
    y_scr[...] = jnp.dot(jnp.concatenate(parts, axis=1), wbuf[slot], preferred_element_type=F32)

    n_extra = []
    for e in range(N_EXPERTS):
        first = lo_ref[i * N_EXPERTS + e]
        covered_end = win_start(e * cap + first) - e * cap + W
        remaining = jnp.maximum(first + cnt_ref[i * N_EXPERTS + e] - covered_end, 0)
        n_extra.append((covered_end, (remaining + COMBINE_STRIDE - 1) // COMBINE_STRIDE))

    @pl.when(functools.reduce(lambda a, b: a + b, [n for _, n in n_extra]) > 0)
    def _():
        for e in range(N_EXPERTS):
            covered_end, n_e = n_extra[e]

            def extra(k, carry, e=e, covered_end=covered_end):
                want = covered_end + k * COMBINE_STRIDE
                st = win_start(e * cap + want)
                cp = pltpu.make_async_copy(out_hbm.at[pl.ds(st, W)], xbuf, sem_x)
                cp.start()
                cp.wait()
                pe = pos_ref[:, e:e + 1]
                hit = (pe - (st - e * cap).astype(F32) == lane) & (pe >= want.astype(F32)) \
                    & (pe < (want + COMBINE_STRIDE).astype(F32))
                y_scr[...] += jnp.dot(hit.astype(BF16), xbuf[...], preferred_element_type=F32)
                return carry

            lax.fori_loop(0, n_e, extra, 0)

    x = x_ref[...] + y_scr[...]
    ms = jnp.mean(x * x, axis=-1, keepdims=True)
    h = (x * lax.rsqrt(ms + EPS) * np_ref[...]).astype(BF16)
    gate = _sigmoid(jnp.dot(h, wg_ref[...], preferred_element_type=F32))
    pp = jnp.dot(p_ref[...].astype(BF16), wp_ref[...], preferred_element_type=F32)
    x = x + gate * pp
    if final:
        ms = jnp.mean(x * x, axis=-1, keepdims=True)
        x = x * lax.rsqrt(ms + EPS) * fn_ref[...]
    o_ref[...] = x


def _ple(x2d, out, pos, bex, p_all, layer, norm_ple, wg, wp, final_norm, final):
    T = x2d.shape[0]
    tm = min(TM_PLE, T)
    n_tiles = T // tm
    p_off = layer * n_tiles
    cap = out.shape[0] // N_EXPERTS
    W = COMBINE_WINDOW
    assert cap >= W and cap % BF16_SUBLANES == 0 and tm % LANES == 0
    lo = bex[:, ::tm // LANES].astype(jnp.int32)
    cnt = jnp.diff(lo, axis=1, append=jnp.full((N_EXPERTS, 1), cap, jnp.int32))
    lo = lo.T.reshape(-1)
    cnt = cnt.T.reshape(-1)
    pos_t = pos.T
    vmem = (2 * (2 * tm * D_MODEL * 4 + tm * LANES * 4 + tm * PLE_DIM * 4 + D_MODEL * D_MODEL * 2
                 + PLE_DIM * D_MODEL * 2) + 2 * N_EXPERTS * W * D_MODEL * 2 + W * D_MODEL * 2
            + tm * N_EXPERTS * W * 2 + 6 * tm * D_MODEL * 4)
    tok = lambda width: pl.BlockSpec((tm, width), lambda i, lo, cnt: (i, 0))
    full = lambda shape: pl.BlockSpec(shape, lambda i, lo, cnt: (0,) * len(shape))
    grid_spec = pltpu.PrefetchScalarGridSpec(
        num_scalar_prefetch=2,
        grid=(n_tiles,),
        in_specs=[tok(D_MODEL), tok(N_EXPERTS),
                  pl.BlockSpec((tm, PLE_DIM), lambda i, lo, cnt: (p_off + i, 0)),
                  full((1, D_MODEL)), full((D_MODEL, D_MODEL)),
                  full((PLE_DIM, D_MODEL)), full((1, D_MODEL)), pl.BlockSpec(memory_space=pl.ANY)],
        out_specs=tok(D_MODEL),
        scratch_shapes=[
            pltpu.VMEM((2, N_EXPERTS * W, D_MODEL), BF16),
            pltpu.VMEM((W, D_MODEL), BF16),
            pltpu.VMEM((tm, D_MODEL), F32),
            pltpu.SemaphoreType.DMA((2,)),
            pltpu.SemaphoreType.DMA,
        ],
    )
    return pl.pallas_call(
        functools.partial(_ple_kernel, final=final, cap=cap, n_tiles=n_tiles),
        grid_spec=grid_spec,
        out_shape=jax.ShapeDtypeStruct((T, D_MODEL), F32),
        compiler_params=pltpu.CompilerParams(
            dimension_semantics=("arbitrary",), vmem_limit_bytes=_vmem_limit(vmem)),
        name="combine_ple",
    )(lo, cnt, x2d, pos_t, p_all, norm_ple, wg, wp, final_norm, out)


def _prep_layer(i, norm_mix, w_in, conv_w, na_rpb, gla_w2, gla_b, gla_norm, w_branch, w_out, norm_ffn, w_router,
                w_exp_gate, w_exp_up, w_exp_down, norm_ple, w_ple_gate, w_ple_proj):
    w = w_in[i]
    sizes = (CONV_WIDTH,) * 3 + (NA_WIDTH,) * 3 + (GLA_KEY_WIDTH,) * 2 + (GLA_VAL_WIDTH,) * 2 + (2 * GLA_RANK,)
    offs = np.concatenate([[0], np.cumsum(sizes)])
    gates_off = int(offs[-1])
    lr_off = int(offs[10])
    w_main = jnp.concatenate([w[:, gates_off:], w[:, :lr_off]], axis=1)
    assert float(np.log2(NA_HEAD_DIM ** -0.5)).is_integer() and float(np.log2(GLA_DK ** -0.5)).is_integer()
    col_scale = np.ones((PROJ_COLS,), np.float32)
    col_scale[COL_NQ:COL_NQ + NA_WIDTH] = NA_HEAD_DIM ** -0.5
    col_scale[COL_GQ:COL_GQ + GLA_KEY_WIDTH] = GLA_DK ** -0.5
    w_main = (w_main * col_scale).astype(BF16)
    w_lr = jnp.zeros((D_MODEL, LANES), F32).at[:, :2 * GLA_RANK].set(w[:, lr_off:gates_off]).astype(BF16)
    return dict(
        norm_mix=norm_mix[i].reshape(1, D_MODEL), w_main=w_main, w_lr=w_lr, conv_w=conv_w[i], na_rpb=na_rpb[i],
        gla_w2=gla_w2[i], gla_b=gla_b[i], gla_norm=gla_norm[i], w_branch=w_branch[i].astype(BF16),
        w_out=w_out[i].astype(BF16), norm_ffn=norm_ffn[i].reshape(1, D_MODEL), w_router=w_router[i],
        w_exp_gate=w_exp_gate[i].astype(BF16), w_exp_up=w_exp_up[i].astype(BF16),
        w_exp_down=w_exp_down[i].astype(BF16), norm_ple=norm_ple[i].reshape(1, D_MODEL),
        w_ple_gate=w_ple_gate[i].astype(BF16), w_ple_proj=w_ple_proj[i].astype(BF16))


def _moe(hp, aff, lp):
    n_tok = hp.shape[0]
    cap = CAPACITY_FACTOR * n_tok // N_EXPERTS
    idx, gate_val, pos, bex = _route(aff, cap)
    out = _moe_ffn(hp, idx, gate_val, lp["w_exp_gate"], lp["w_exp_up"], lp["w_exp_down"])
    return out, pos, bex


def _trunk(x, p, layers, final_norm):
    bsz, L, _ = x.shape
    T = bsz * L
    x2d = x.reshape(T, D_MODEL)
    fn = final_norm.reshape(1, D_MODEL)
    for i, lp in enumerate(layers):
        proj, lr = _inproj(x2d, lp["norm_mix"], lp["w_main"], lp["w_lr"])
        y_b = _na_attention(proj, lp["na_rpb"], bsz, L)
        y_c = _gla(proj, lr, lp["gla_w2"], lp["gla_b"], lp["gla_norm"], bsz, L)
        x2d, h, aff = _merge(x2d, proj, y_b, y_c, lp["conv_w"], lp["w_branch"], lp["w_out"], lp["norm_ffn"],
                             lp["w_router"], L)
        out, pos, bex = _moe(h, aff, lp)
        x2d = _ple(x2d, out, pos, bex, p.reshape(-1, PLE_DIM), i, lp["norm_ple"], lp["w_ple_gate"], lp["w_ple_proj"], fn,
                   final=(i == len(layers) - 1))
    return x2d.reshape(bsz, L, D_MODEL)


def kernel(x_prompt, x_sample, p_prompt, p_sample, norm_mix, w_in, conv_w, na_rpb, gla_w2, gla_b, gla_norm, w_branch, w_out, norm_ffn, w_router, w_exp_gate, w_exp_up, w_exp_down, norm_ple, w_ple_gate, w_ple_proj, final_norm):
    depth = w_in.shape[0]
    layers = [_prep_layer(i, norm_mix, w_in, conv_w, na_rpb, gla_w2, gla_b, gla_norm, w_branch, w_out, norm_ffn,
                          w_router, w_exp_gate, w_exp_up, w_exp_down, norm_ple, w_ple_gate, w_ple_proj)
              for i in range(depth)]
    y_prompt = _trunk(x_prompt, p_prompt, layers, final_norm)
    y_sample = _trunk(x_sample, p_sample, layers, final_norm)
    return (y_prompt, y_sample)
```

```python
import functools

import numpy as np
import jax
import jax.numpy as jnp
from jax import lax
from jax.experimental import pallas as pl
from jax.experimental.pallas import tpu as pltpu

F32 = jnp.float32
BF16 = jnp.bfloat16

D_MODEL = 1024
GRID_W = 64
PLE_DIM = 256
EPS = 1e-6
CONV_WIDTH = 512
CONV_K = 3
NA_HEADS = 8
NA_HEAD_DIM = 64
NA_WIDTH = NA_HEADS * NA_HEAD_DIM
WIN_ROWS = 8
WIN_COLS = 16
GLA_HEADS = 4
GLA_DK = 64
GLA_DV = 128
GLA_KEY_WIDTH = GLA_HEADS * GLA_DK
GLA_VAL_WIDTH = GLA_HEADS * GLA_DV
GLA_RANK = 16
GLA_GATE_NORM = 16.0
GLA_CHUNK = 64
N_BRANCH = 3
N_EXPERTS = 16
EXPERT_FF = 2048
CAPACITY_FACTOR = 2

V7X_VMEM_BYTES = 64 * 1024 * 1024
LANES = 128
BF16_SUBLANES = 16

PROJ_TN = 512
COL_GATES = 0
COL_AB = 3 * D_MODEL
COL_AC = COL_AB + CONV_WIDTH
COL_AX = COL_AC + CONV_WIDTH
COL_NQ = COL_AX + CONV_WIDTH
COL_NK = COL_NQ + NA_WIDTH
COL_NV = COL_NK + NA_WIDTH
COL_GQ = COL_NV + NA_WIDTH
COL_GK = COL_GQ + GLA_KEY_WIDTH
COL_GV = COL_GK + GLA_KEY_WIDTH
COL_GG = COL_GV + GLA_VAL_WIDTH
PROJ_COLS = COL_GG + GLA_VAL_WIDTH
N_GATE_BLOCKS = (3 * D_MODEL) // PROJ_TN

TM_INPROJ = 512
NA_TILE_ROWS = 4
NA_TQ = NA_TILE_ROWS * GRID_W
GLA_TC = 512
TM_MERGE = 256
TM_FFN = 512
TM_PLE = 512

NEG_INF = -1e30
LOG2E = float(np.log2(np.e))


def _vmem_limit(nbytes):
    return int(min(V7X_VMEM_BYTES - 6 * 1024 * 1024, max(32 * 1024 * 1024, nbytes + 8 * 1024 * 1024)))


def _sigmoid(x):
    return 1.0 / (1.0 + jnp.exp(-x))


def _split3(a):
    hi = a.astype(BF16)
    r1 = a - hi.astype(F32)
    mid = r1.astype(BF16)
    lo = (r1 - mid.astype(F32)).astype(BF16)
    return hi, mid, lo


def _inproj_kernel(x_ref, nw_ref, wlr_ref, w_hbm, proj_ref, lr_ref, w_vmem, sem):
    @pl.when(pl.program_id(0) == 0)
    def _():
        cp = pltpu.make_async_copy(w_hbm, w_vmem, sem)
        cp.start()
        cp.wait()

    x = x_ref[...]
    ms = jnp.mean(x * x, axis=-1, keepdims=True)
    hb = (x * lax.rsqrt(ms + EPS) * nw_ref[...]).astype(BF16)
    lr_ref[...] = jnp.dot(hb, wlr_ref[...], preferred_element_type=F32)
    for j in range(PROJ_COLS // PROJ_TN):
        cs = slice(j * PROJ_TN, (j + 1) * PROJ_TN)
        acc = jnp.dot(hb, w_vmem[:, cs], preferred_element_type=F32)
        if j < N_GATE_BLOCKS:
            acc = _sigmoid(acc)
        proj_ref[:, cs] = acc.astype(BF16)


def _inproj(x2d, norm_w, w_main, w_lr):
    T = x2d.shape[0]
    tm = min(TM_INPROJ, T)
    vmem = (2 * (tm * D_MODEL * 4 + D_MODEL * LANES * 2 + tm * PROJ_COLS * 2 + tm * LANES * 4)
            + D_MODEL * PROJ_COLS * 2 + tm * D_MODEL * 2 + 4 * tm * PROJ_TN * 4)
    return pl.pallas_call(
        _inproj_kernel,
        grid=(T // tm,),
        in_specs=[
            pl.BlockSpec((tm, D_MODEL), lambda i: (i, 0)),
            pl.BlockSpec((1, D_MODEL), lambda i: (0, 0)),
            pl.BlockSpec((D_MODEL, LANES), lambda i: (0, 0)),
            pl.BlockSpec(memory_space=pl.ANY),
        ],
        out_specs=[
            pl.BlockSpec((tm, PROJ_COLS), lambda i: (i, 0)),
            pl.BlockSpec((tm, LANES), lambda i: (i, 0)),
        ],
        out_shape=[
            jax.ShapeDtypeStruct((T, PROJ_COLS), BF16),
            jax.ShapeDtypeStruct((T, LANES), F32),
        ],
        scratch_shapes=[pltpu.VMEM((D_MODEL, PROJ_COLS), BF16), pltpu.SemaphoreType.DMA],
        compiler_params=pltpu.CompilerParams(
            dimension_semantics=("arbitrary",), vmem_limit_bytes=_vmem_limit(vmem)),
        name="inproj",
    )(x2d, norm_w, w_lr, w_main)


def _na_tables(rows):
    R = NA_TILE_ROWS
    J = rows // R
    kr = min(WIN_ROWS, rows)
    qr_rel = np.arange(NA_TQ) // GRID_W
    qc = np.arange(NA_TQ) % GRID_W
    kk = np.arange(3 * NA_TQ)
    kblk = kk // NA_TQ
    kwithin = kk % NA_TQ
    kc = kwithin % GRID_W
    col_start = np.clip(qc - WIN_COLS // 2, 0, GRID_W - WIN_COLS)
    col_ok = (kc[None, :] >= col_start[:, None]) & (kc[None, :] < col_start[:, None] + WIN_COLS)
    dc = np.clip(kc[None, :] - qc[:, None] + WIN_COLS - 1, 0, 2 * WIN_COLS - 2)
    types, keys, tile_type = [], {}, []
    for j in range(J):
        blocks = np.array([max(j - 1, 0), j, min(j + 1, J - 1)])
        blk_ok = np.array([j - 1 >= 0, True, j + 1 <= J - 1])
        krow = blocks[kblk] * R + kwithin // GRID_W
        qrow = j * R + qr_rel
        rs = np.clip(qrow - kr // 2, 0, rows - kr)
        row_ok = (krow[None, :] >= rs[:, None]) & (krow[None, :] < rs[:, None] + kr) & blk_ok[kblk][None, :]
        mask = row_ok & col_ok
        dr = np.clip(krow[None, :] - qrow[:, None] + WIN_ROWS - 1, 0, 2 * WIN_ROWS - 2)
        dr = np.where(mask, dr, 0)
        key = mask.tobytes() + dr.astype(np.int8).tobytes()
        if key not in keys:
            keys[key] = len(types)
            types.append((mask, dr, np.where(mask, dc, 0)))
        tile_type.append(keys[key])
    mask = np.stack([t[0] for t in types])
    dr = np.stack([t[1] for t in types]).astype(np.int32)
    dcs = np.stack([t[2] for t in types]).astype(np.int32)
    return mask, dr, dcs, np.array(tile_type, np.int32)


def _na_bias(rpb, mask, dr):
    n_types = mask.shape[0]
    W = GRID_W
    qc = np.arange(W)[:, None]
    kc = np.arange(W)[None, :]
    col_start = np.clip(qc - WIN_COLS // 2, 0, W - WIN_COLS)
    col_ok = (kc >= col_start) & (kc < col_start + WIN_COLS)
    n_dc = 2 * WIN_COLS - 1
    lead = W - WIN_COLS
    padded = jnp.pad(rpb, ((0, 0), (0, 0), (lead, 2 * W - 1 - lead - n_dc)), constant_values=NEG_INF)
    toep = jnp.stack([padded[:, :, W - 1 - q:2 * W - 1 - q] for q in range(W)], axis=2)
    toep = jnp.where(col_ok[None, None], toep, NEG_INF)
    dead = jnp.full((NA_HEADS, 1, W, W), NEG_INF, F32)
    blocks = jnp.concatenate([toep, dead], axis=1)
    n_dead = toep.shape[1]
    tables = []
    for t in range(n_types):
        ids = np.where(mask[t, ::W, ::W], dr[t, ::W, ::W], n_dead)
        rows_ = [jnp.concatenate([blocks[:, int(b)] for b in ids_row], axis=-1) for ids_row in ids]
        tables.append(jnp.concatenate(rows_, axis=1))
    return jnp.stack(tables)


def _na_kernel(tid_ref, q_ref, kp_ref, kc_ref, kn_ref, vp_ref, vc_ref, vn_ref, bias_ref, o_ref):
    del tid_ref
    q = q_ref[...]
    k = jnp.concatenate([kp_ref[...], kc_ref[...], kn_ref[...]], axis=0)
    v = jnp.concatenate([vp_ref[...], vc_ref[...], vn_ref[...]], axis=0)
    outs = []
    for h in range(NA_HEADS):
        sl = slice(h * NA_HEAD_DIM, (h + 1) * NA_HEAD_DIM)
        s = lax.dot_general(q[:, sl], k[:, sl], (((1,), (1,)), ((), ())), preferred_element_type=F32)
        s = s + bias_ref[0, h]
        m = jnp.max(s, axis=-1, keepdims=True)
        p = jnp.exp2(s - m)
        l = jnp.sum(p, axis=-1, keepdims=True)
        o = jnp.dot(p.astype(BF16), v[:, sl], preferred_element_type=F32)
        outs.append(o * (1.0 / l))
    o_ref[...] = jnp.concatenate(outs, axis=-1).astype(BF16)


def _na_attention(proj, rpb, bsz, L):
    T = proj.shape[0]
    rows = L // GRID_W
    J = rows // NA_TILE_ROWS
    mask, dr, _, tile_type = _na_tables(rows)
    bias = _na_bias(rpb.astype(F32) * LOG2E, mask, dr)
    qb, kb, vb = COL_NQ // NA_WIDTH, COL_NK // NA_WIDTH, COL_NV // NA_WIDTH

    def prev(b, j, tid):
        return b * J + jnp.maximum(j - 1, 0)

    def nxt(b, j, tid):
        return b * J + jnp.minimum(j + 1, J - 1)

    blk = (NA_TQ, NA_WIDTH)
    vmem = 2 * (8 * NA_TQ * NA_WIDTH * 2 + NA_HEADS * NA_TQ * 3 * NA_TQ * 4) + 8 * NA_TQ * 3 * NA_TQ * 4
    grid_spec = pltpu.PrefetchScalarGridSpec(
        num_scalar_prefetch=1,
        grid=(bsz, J),
        in_specs=[
            pl.BlockSpec(blk, lambda b, j, tid: (b * J + j, qb)),
            pl.BlockSpec(blk, lambda b, j, tid: (prev(b, j, tid), kb)),
            pl.BlockSpec(blk, lambda b, j, tid: (b * J + j, kb)),
            pl.BlockSpec(blk, lambda b, j, tid: (nxt(b, j, tid), kb)),
            pl.BlockSpec(blk, lambda b, j, tid: (prev(b, j, tid), vb)),
            pl.BlockSpec(blk, lambda b, j, tid: (b * J + j, vb)),
            pl.BlockSpec(blk, lambda b, j, tid: (nxt(b, j, tid), vb)),
            pl.BlockSpec((1, NA_HEADS, NA_TQ, 3 * NA_TQ), lambda b, j, tid: (tid[j], 0, 0, 0)),
        ],
        out_specs=pl.BlockSpec(blk, lambda b, j, tid: (b * J + j, 0)),
    )
    return pl.pallas_call(
        _na_kernel,
        grid_spec=grid_spec,
        out_shape=jax.ShapeDtypeStruct((T, NA_WIDTH), BF16),
        compiler_params=pltpu.CompilerParams(
            dimension_semantics=("arbitrary", "arbitrary"), vmem_limit_bytes=_vmem_limit(vmem)),
        name="na_attention",
    )(jnp.asarray(tile_type), proj, proj, proj, proj, proj, proj, proj, bias)


def _gla_kernel(q_ref, k_ref, v_ref, gg_ref, lr_ref, cum_ref, w2_ref, gb_ref, gn_ref, o_ref, ofwd_scr, st_scr,
                *, nt):
    p = pl.program_id(1)
    j = pl.program_id(2)
    c = GLA_CHUNK
    n_chunks = GLA_TC // c
    H = GLA_HEADS

    @pl.when(j == 0)
    def _():
        st_scr[...] = jnp.zeros_like(st_scr)

    lane_head = lax.broadcasted_iota(jnp.int32, (1, GLA_KEY_WIDTH), 1) // GLA_DK
    ri = lax.broadcasted_iota(jnp.int32, (H * c, c), 0) % c
    ci = lax.broadcasted_iota(jnp.int32, (H * c, c), 1)

    def sweep(direction):
        tri = (ci <= ri) if direction == 0 else (ci >= ri)
        z = lr_ref[...].astype(BF16)
        xg = jnp.dot(z, w2_ref[direction], preferred_element_type=F32) + gb_ref[direction]
        la = (jnp.minimum(xg, 0.0) - jnp.log1p(jnp.exp(-jnp.abs(xg)))) * (1.0 / GLA_GATE_NORM)
        hi, mid, lo = _split3(la)
        cum = cum_ref[direction]
        b_all = (jnp.dot(cum, hi, preferred_element_type=F32)
                 + jnp.dot(cum, mid, preferred_element_type=F32)
                 + jnp.dot(cum, lo, preferred_element_type=F32))
        k_all = k_ref[...].astype(F32)
        qd_all = q_ref[...].astype(F32) * jnp.exp(b_all)
        kd_all = (k_all * jnp.exp(-b_all)).astype(BF16)
        st = st_scr[...]
        order = range(n_chunks) if direction == 0 else range(n_chunks - 1, -1, -1)
        for ch in order:
            rs = slice(ch * c, (ch + 1) * c)
            b = b_all[rs]
            b_last = b[c - 1:c, :] if direction == 0 else b[0:1, :]
            v = v_ref[rs, :]
            qd = qd_all[rs]
            kd = kd_all[rs]
            krem = (k_all[rs] * jnp.exp(b_last - b)).astype(BF16)
            st_b = st.astype(BF16)
            nt_dims = (((1,), (1,)), ((), ()))
            qs = jnp.concatenate([jnp.where(lane_head == h, qd, 0.0) for h in range(H)], axis=0).astype(BF16)
            att = lax.dot_general(qs, kd, nt_dims, preferred_element_type=F32)
            att = jnp.where(tri, att, 0.0).astype(BF16)
            intra = jnp.dot(att, v, preferred_element_type=F32)
            inter = lax.dot_general(qs, st_b, nt_dims, preferred_element_type=F32)
            o_chunk = jnp.concatenate(
                [intra[h * c:(h + 1) * c, h * GLA_DV:(h + 1) * GLA_DV] + inter[h * c:(h + 1) * c]
                 for h in range(H)], axis=-1)
            kv = lax.dot_general(v, krem, (((0,), (0,)), ((), ())), preferred_element_type=F32)
            st = st * jnp.exp(b_last)
            for h in range(H):
                st = st + jnp.where(lane_head == h, kv[h * GLA_DV:(h + 1) * GLA_DV, :], 0.0)
            yield ch, o_chunk
        st_scr[...] = st

    @pl.when(p == 0)
    def _():
        for ch, o_chunk in sweep(0):
            start = pl.multiple_of(j * GLA_TC + ch * c, c)
            ofwd_scr[pl.ds(start, c), :] = o_chunk

    @pl.when(p == 1)
    def _():
        t = nt - 1 - j
        gn = gn_ref[...]
        for ch, o_chunk in sweep(1):
            start = pl.multiple_of(t * GLA_TC + ch * c, c)
            o = ofwd_scr[pl.ds(start, c), :] + o_chunk
            normed = []
            for h in range(H):
                oh = o[:, h * GLA_DV:(h + 1) * GLA_DV]
                ms = jnp.mean(oh * oh, axis=-1, keepdims=True)
                normed.append(oh * lax.rsqrt(ms + EPS))
            on = jnp.concatenate(normed, axis=-1) * gn
            g = gg_ref[ch * c:(ch + 1) * c, :].astype(F32)
            o_ref[ch * c:(ch + 1) * c, :] = (g * _sigmoid(g) * on).astype(BF16)


def _gla(proj, lr, gla_w2, gla_b, gla_norm, bsz, L):
    T = proj.shape[0]
    nt = L // GLA_TC
    w2p = jnp.zeros((2, LANES, GLA_KEY_WIDTH), F32)
    w2p = w2p.at[0, 0:GLA_RANK].set(gla_w2[0]).at[1, GLA_RANK:2 * GLA_RANK].set(gla_w2[1]).astype(BF16)
    gb = gla_b.reshape(2, 1, GLA_KEY_WIDTH).astype(F32)
    gn = gla_norm.reshape(1, GLA_VAL_WIDTH).astype(F32)
    r = np.arange(GLA_TC)
    same = (r[:, None] // GLA_CHUNK) == (r[None, :] // GLA_CHUNK)
    cum = jnp.asarray(np.stack([same & (r[None, :] <= r[:, None]), same & (r[None, :] >= r[:, None])]), BF16)

    def tile(b, p, j):
        return b * nt + jnp.where(p == 0, j, nt - 1 - j)

    def otile(b, p, j):
        return b * nt + jnp.where(p == 0, nt - 1, nt - 1 - j)

    kq, kk = COL_GQ // GLA_KEY_WIDTH, COL_GK // GLA_KEY_WIDTH
    kv, kg = COL_GV // GLA_VAL_WIDTH, COL_GG // GLA_VAL_WIDTH
    vmem = (2 * (2 * GLA_TC * GLA_KEY_WIDTH * 2 + 3 * GLA_TC * GLA_VAL_WIDTH * 2 + GLA_TC * LANES * 4
                 + 2 * GLA_TC * GLA_TC * 2)
            + L * GLA_VAL_WIDTH * 4 + GLA_DV * GLA_KEY_WIDTH * 4 + 16 * GLA_TC * GLA_KEY_WIDTH * 4)
    return pl.pallas_call(
        functools.partial(_gla_kernel, nt=nt),
        grid=(bsz, 2, nt),
        in_specs=[
            pl.BlockSpec((GLA_TC, GLA_KEY_WIDTH), lambda b, p, j: (tile(b, p, j), kq)),
            pl.BlockSpec((GLA_TC, GLA_KEY_WIDTH), lambda b, p, j: (tile(b, p, j), kk)),
            pl.BlockSpec((GLA_TC, GLA_VAL_WIDTH), lambda b, p, j: (tile(b, p, j), kv)),
            pl.BlockSpec((GLA_TC, GLA_VAL_WIDTH), lambda b, p, j: (tile(b, p, j), kg)),
            pl.BlockSpec((GLA_TC, LANES), lambda b, p, j: (tile(b, p, j), 0)),
            pl.BlockSpec((2, GLA_TC, GLA_TC), lambda b, p, j: (0, 0, 0)),
            pl.BlockSpec((2, LANES, GLA_KEY_WIDTH), lambda b, p, j: (0, 0, 0)),
            pl.BlockSpec((2, 1, GLA_KEY_WIDTH), lambda b, p, j: (0, 0, 0)),
            pl.BlockSpec((1, GLA_VAL_WIDTH), lambda b, p, j: (0, 0)),
        ],
        out_specs=pl.BlockSpec((GLA_TC, GLA_VAL_WIDTH), lambda b, p, j: (otile(b, p, j), 0)),
        out_shape=jax.ShapeDtypeStruct((T, GLA_VAL_WIDTH), BF16),
        scratch_shapes=[pltpu.VMEM((L, GLA_VAL_WIDTH), F32), pltpu.VMEM((GLA_DV, GLA_KEY_WIDTH), F32)],
        compiler_params=pltpu.CompilerParams(
            dimension_semantics=("arbitrary", "arbitrary", "arbitrary"), vmem_limit_bytes=_vmem_limit(vmem)),
        name="gla",
    )(proj, proj, proj, proj, lr, cum, w2p, gb, gn)


def _merge_kernel(x_ref, ga_ref, gb_ref, gc_ref, ab_ref, ac_ref, ax_ref, acp_ref, axp_ref, acn_ref, axn_ref,
                  yb_ref, yc_ref, cw_ref, wb_ref, wo_ref, nf_ref, wr_ref, xo_ref, h_ref, aff_ref, *, tiles_per_seq):
    i = pl.program_id(0)
    tm = x_ref.shape[0]
    pos = i % tiles_per_seq
    has_prev = jnp.where(pos == 0, 0.0, 1.0)
    has_next = jnp.where(pos == tiles_per_seq - 1, 0.0, 1.0)
    u = ac_ref[...].astype(F32) * ax_ref[...].astype(F32)
    up = (acp_ref[...].astype(F32) * axp_ref[...].astype(F32))[BF16_SUBLANES - 1:BF16_SUBLANES, :] * has_prev
    un = (acn_ref[...].astype(F32) * axn_ref[...].astype(F32))[0:1, :] * has_next
    ridx = lax.broadcasted_iota(jnp.int32, (tm, 1), 0)
    u_prev = jnp.where(ridx == 0, up, pltpu.roll(u, 1, axis=0))
    u_next = jnp.where(ridx == tm - 1, un, pltpu.roll(u, tm - 1, axis=0))
    cw = cw_ref[...]
    y_a = ab_ref[...].astype(F32) * (u_prev * cw[0:1, :] + u * cw[1:2, :] + u_next * cw[2:3, :])
    m = ga_ref[...].astype(F32) * jnp.dot(y_a.astype(BF16), wb_ref[0], preferred_element_type=F32)
    m = m + gb_ref[...].astype(F32) * jnp.dot(yb_ref[...], wb_ref[1], preferred_element_type=F32)
    m = m + gc_ref[...].astype(F32) * jnp.dot(yc_ref[...], wb_ref[2], preferred_element_type=F32)
    x_new = x_ref[...] + jnp.dot(m.astype(BF16), wo_ref[...], preferred_element_type=F32)
    xo_ref[...] = x_new
    ms = jnp.mean(x_new * x_new, axis=-1, keepdims=True)
    h = x_new * lax.rsqrt(ms + EPS) * nf_ref[...]
    bits = lax.bitcast_convert_type(h.astype(BF16).astype(F32), jnp.uint32)
    half = D_MODEL // 2
    h_ref[...] = (bits[:, :half] >> 16) | (bits[:, half:] & jnp.uint32(0xFFFF0000))
    h0, h1, _ = _split3(h)
    w0, w1 = wr_ref[0], wr_ref[1]
    logits = (jnp.dot(h0, w0, preferred_element_type=F32) + jnp.dot(h0, w1, preferred_element_type=F32)
              + jnp.dot(h1, w0, preferred_element_type=F32))
    lane = lax.broadcasted_iota(jnp.int32, logits.shape, 1)
    logits = jnp.where(lane < N_EXPERTS, logits, NEG_INF)
    mx = jnp.max(logits, axis=-1, keepdims=True)
    e = jnp.exp(logits - mx)
    aff_ref[...] = e / jnp.sum(e, axis=-1, keepdims=True)


def _merge(x2d, proj, y_b, y_c, conv_w, wb, wo, norm_ffn, w_router, L):
    T = x2d.shape[0]
    tm = min(TM_MERGE, L)
    tiles_per_seq = L // tm
    n16 = T // BF16_SUBLANES
    per16 = tm // BF16_SUBLANES
    wr = jnp.zeros((D_MODEL, LANES), F32).at[:, :N_EXPERTS].set(w_router)
    wr = jnp.stack(_split3(wr)[:2])
    cb = CONV_WIDTH
    vmem = 2 * (2 * tm * D_MODEL * 4 + 3 * tm * D_MODEL * 2 + 5 * tm * cb * 2 + 4 * BF16_SUBLANES * cb * 2
                + 3 * cb * D_MODEL * 2 + D_MODEL * D_MODEL * 2 + 3 * D_MODEL * LANES * 2
                + tm * D_MODEL * 2 + tm * LANES * 4) + 8 * tm * D_MODEL * 4
    tok = lambda width, col: pl.BlockSpec((tm, width), lambda i: (i, col // width))
    halo_p = lambda col: pl.BlockSpec((BF16_SUBLANES, cb), lambda i: (jnp.maximum(i * per16 - 1, 0), col // cb))
    halo_n = lambda col: pl.BlockSpec((BF16_SUBLANES, cb),
                                      lambda i: (jnp.minimum((i + 1) * per16, n16 - 1), col // cb))
    full = lambda shape: pl.BlockSpec(shape, lambda i: (0,) * len(shape))
    return pl.pallas_call(
        functools.partial(_merge_kernel, tiles_per_seq=tiles_per_seq),
        grid=(T // tm,),
        in_specs=[
            tok(D_MODEL, 0),
            tok(D_MODEL, COL_GATES), tok(D_MODEL, COL_GATES + D_MODEL), tok(D_MODEL, COL_GATES + 2 * D_MODEL),
            tok(cb, COL_AB), tok(cb, COL_AC), tok(cb, COL_AX),
            halo_p(COL_AC), halo_p(COL_AX), halo_n(COL_AC), halo_n(COL_AX),
            tok(NA_WIDTH, 0), tok(GLA_VAL_WIDTH, 0),
            full((CONV_K, cb)), full((N_BRANCH, cb, D_MODEL)), full((D_MODEL, D_MODEL)),
            full((1, D_MODEL)), full((2, D_MODEL, LANES)),
        ],
        out_specs=[tok(D_MODEL, 0), tok(D_MODEL // 2, 0), tok(LANES, 0)],
        out_shape=[
            jax.ShapeDtypeStruct((T, D_MODEL), F32),
            jax.ShapeDtypeStruct((T, D_MODEL // 2), jnp.uint32),
            jax.ShapeDtypeStruct((T, LANES), F32),
        ],
        compiler_params=pltpu.CompilerParams(
            dimension_semantics=("arbitrary",), vmem_limit_bytes=_vmem_limit(vmem)),
        name="merge",
    )(x2d, proj, proj, proj, proj, proj, proj, proj, proj, proj, proj, y_b, y_c,
      conv_w, wb, wo, norm_ffn, wr)


DMA_UNROLL = 8


def _moe_kernel(idx_ref, idxn_ref, g_ref, wg_ref, wu_ref, wd_ref, hp_hbm, o_ref, hbuf, sem_h, *, n_steps):
    tm = hbuf.shape[1]
    s = pl.program_id(0)
    slot = s % 2

    def gather_rows(idx, sl):
        def body(g, c):
            for k in range(DMA_UNROLL):
                r = g * DMA_UNROLL + k
                t = idx[0, 0, r]
                pltpu.make_async_copy(hp_hbm.at[pl.ds(t, 1)], hbuf.at[sl, pl.ds(r, 1)],
                                      sem_h.at[sl]).start(priority=k % 2)
            return c
        lax.fori_loop(0, tm // DMA_UNROLL, body, 0)

    @pl.when(s == 0)
    def _():
        gather_rows(idx_ref, slot)

    @pl.when(s + 1 < n_steps)
    def _():
        gather_rows(idxn_ref, 1 - slot)

    pltpu.make_async_copy(hp_hbm.at[pl.ds(0, tm)], hbuf.at[slot], sem_h.at[slot]).wait()
    w = hbuf[slot]
    lo = lax.bitcast_convert_type(w << 16, F32)
    hi = lax.bitcast_convert_type(w & jnp.uint32(0xFFFF0000), F32)
    xe = jnp.concatenate([lo, hi], axis=1).astype(BF16)
    half = EXPERT_FF // 2
    acc = None
    for f in range(2):
        fs = slice(f * half, (f + 1) * half)
        a = jnp.dot(xe, wg_ref[0, :, fs], preferred_element_type=F32)
        b = jnp.dot(xe, wu_ref[0, :, fs], preferred_element_type=F32)
        he = (a * _sigmoid(a) * b).astype(BF16)
        part = jnp.dot(he, wd_ref[0, fs, :], preferred_element_type=F32)
        acc = part if acc is None else acc + part
    o_ref[...] = (acc * g_ref[0]).astype(BF16)


def _moe_ffn(hp, idx, gate, wg, wu, wd):
    E, cap = idx.shape
    tm = min(TM_FFN, cap)
    assert tm % DMA_UNROLL == 0
    nt = cap // tm
    n_steps = E * nt
    half = D_MODEL // 2
    idx3 = idx.reshape(n_steps, 1, tm).astype(jnp.int32)
    g3 = gate.reshape(n_steps, tm, 1).astype(F32)
    vmem = (2 * (3 * D_MODEL * EXPERT_FF * 2 + tm * LANES * 4 + tm * D_MODEL * 2) + 2 * tm * half * 4
            + 3 * tm * EXPERT_FF * 4)
    smem = lambda f: pl.BlockSpec((1, 1, tm), f, memory_space=pltpu.SMEM)
    wspec = lambda shape: pl.BlockSpec((1,) + shape, lambda s: (s // nt, 0, 0))
    return pl.pallas_call(
        functools.partial(_moe_kernel, n_steps=n_steps),
        grid=(n_steps,),
        in_specs=[
            smem(lambda s: (s, 0, 0)),
            smem(lambda s: (jnp.minimum(s + 1, n_steps - 1), 0, 0)),
            pl.BlockSpec((1, tm, 1), lambda s: (s, 0, 0)),
            wspec((D_MODEL, EXPERT_FF)), wspec((D_MODEL, EXPERT_FF)), wspec((EXPERT_FF, D_MODEL)),
            pl.BlockSpec(memory_space=pl.ANY),
        ],
        out_specs=pl.BlockSpec((tm, D_MODEL), lambda s: (s, 0)),
        out_shape=jax.ShapeDtypeStruct((E * cap, D_MODEL), BF16),
        scratch_shapes=[pltpu.VMEM((2, tm, half), jnp.uint32), pltpu.SemaphoreType.DMA((2,))],
        compiler_params=pltpu.CompilerParams(
            dimension_semantics=("arbitrary",), vmem_limit_bytes=_vmem_limit(vmem)),
        name="moe_ffn",
    )(idx3, idx3, g3, wg, wu, wd, hp)


ROUTE_SLOT_CHUNK = 1024


def _select_kernel(a_ref, sel_ref, *, cap):
    keys = lax.bitcast_convert_type(a_ref[...], jnp.int32)
    n_exp, n_tok = keys.shape
    capf = jnp.float32(cap)

    def count(pred):
        return jnp.sum(pred.astype(F32), axis=1, keepdims=True)

    def value_bit(b, lo):
        cand = lo | jnp.left_shift(jnp.int32(1), 30 - b)
        return jnp.where(count(keys >= cand) >= capf, cand, lo)

    thr = lax.fori_loop(0, 31, value_bit, jnp.zeros((n_exp, 1), jnp.int32))
    gt = keys > thr
    eq = keys == thr
    need = capf - count(gt)
    tok = lax.broadcasted_iota(jnp.int32, keys.shape, 1)
    nbits = int(n_tok).bit_length()

    def index_bit(b, m):
        cand = m | jnp.left_shift(jnp.int32(1), nbits - 1 - b)
        ok = (cand <= n_tok) & (count(eq & (tok < cand)) <= need)
        return jnp.where(ok, cand, m)

    m = lax.fori_loop(0, nbits, index_bit, jnp.zeros((n_exp, 1), jnp.int32))
    sel_ref[...] = (gt | (eq & (tok < m))).astype(F32)


def _compact_kernel(sel_ref, aff_ref, idx_ref, gate_ref, pos_ref, bex_ref, *, cap, n_tok):
    sel = sel_ref[0]
    nb = sel.shape[0]
    ch = min(ROUTE_SLOT_CHUNK, cap)
    nt_dims = (((1,), (1,)), ((), ()))
    r128 = lax.broadcasted_iota(jnp.int32, (LANES, LANES), 0)
    c128 = lax.broadcasted_iota(jnp.int32, (LANES, LANES), 1)
    rj = lax.broadcasted_iota(jnp.int32, (nb, nb), 0)
    cj = lax.broadcasted_iota(jnp.int32, (nb, nb), 1)
    selb = sel.astype(BF16)
    cin = jnp.dot(selb, (r128 <= c128).astype(BF16), preferred_element_type=F32)
    totb = jnp.broadcast_to(cin[:, LANES - 1:LANES], (nb, LANES)).astype(BF16)
    bex = jnp.dot((cj < rj).astype(BF16), totb, preferred_element_type=F32)
    tot_row = lax.dot_general(jnp.ones((8, LANES), BF16), selb, nt_dims, preferred_element_type=F32)
    binc_row = jnp.dot(tot_row.astype(BF16), (rj <= cj).astype(BF16), preferred_element_type=F32)[0:1]
    pos_ref[0] = jnp.where(sel > 0.5, bex + cin - 1.0, -1.0)
    bex_ref[0] = bex[:, 0:1]
    bex_hi = jnp.floor(bex * (1.0 / 256.0))
    bex_lo = bex - 256.0 * bex_hi
    a_hi, a_mid, a_lo = _split3(aff_ref[0])
    table = jnp.concatenate([cin.astype(BF16), bex_hi.astype(BF16), bex_lo.astype(BF16), a_hi, a_mid, a_lo], axis=1)
    lane_nb = lax.broadcasted_iota(jnp.int32, (ch, nb), 1).astype(F32)
    lane128 = lax.broadcasted_iota(jnp.int32, (ch, LANES), 1).astype(F32)
    for c in range(cap // ch):
        s = (lax.broadcasted_iota(jnp.int32, (ch, 1), 0) + c * ch).astype(F32)
        blk = jnp.sum((binc_row <= s).astype(F32), axis=1, keepdims=True)
        onehot = (lane_nb == blk).astype(BF16)
        g = jnp.dot(onehot, table, preferred_element_type=F32)
        cinrow = g[:, 0:LANES]
        before = g[:, LANES:2 * LANES] * 256.0 + g[:, 2 * LANES:3 * LANES]
        affrow = g[:, 3 * LANES:4 * LANES] + g[:, 4 * LANES:5 * LANES] + g[:, 5 * LANES:6 * LANES]
        off = jnp.sum((cinrow <= s - before).astype(F32), axis=1, keepdims=True)
        idx = jnp.minimum(blk * float(LANES) + off, float(n_tok - 1))
        gate = jnp.sum(jnp.where(lane128 == off, affrow, 0.0), axis=1, keepdims=True)
        idx_t = jnp.transpose(jnp.broadcast_to(idx, (ch, LANES)))[0:1]
        gate_t = jnp.transpose(jnp.broadcast_to(gate, (ch, LANES)))[0:1]
        idx_ref[0, :, c * ch:(c + 1) * ch] = idx_t.astype(jnp.int32)
        gate_ref[0, :, c * ch:(c + 1) * ch] = gate_t


def _route(aff, cap):
    n_tok = aff.shape[0]
    nb = n_tok // LANES
    a_t = aff[:, :N_EXPERTS].T
    sel = pl.pallas_call(
        functools.partial(_select_kernel, cap=cap),
        out_shape=jax.ShapeDtypeStruct((N_EXPERTS, n_tok), F32),
        compiler_params=pltpu.CompilerParams(vmem_limit_bytes=_vmem_limit(12 * N_EXPERTS * n_tok * 4)),
        name="route_select",
    )(a_t)
    ch = min(ROUTE_SLOT_CHUNK, cap)
    vmem = 2 * (2 * nb * LANES * 4 + 2 * cap * 4) + 2 * nb * nb * 4 + nb * 6 * LANES * 4 + ch * (nb + 12 * LANES) * 4
    blk = pl.BlockSpec((1, nb, LANES), lambda e: (e, 0, 0))
    oblk = pl.BlockSpec((1, 1, cap), lambda e: (e, 0, 0))
    idx, gate, pos, bex = pl.pallas_call(
        functools.partial(_compact_kernel, cap=cap, n_tok=n_tok),
        grid=(N_EXPERTS,),
        in_specs=[blk, blk],
        out_specs=[oblk, oblk, blk, pl.BlockSpec((1, nb, 1), lambda e: (e, 0, 0))],
        out_shape=[jax.ShapeDtypeStruct((N_EXPERTS, 1, cap), jnp.int32),
                   jax.ShapeDtypeStruct((N_EXPERTS, 1, cap), F32),
                   jax.ShapeDtypeStruct((N_EXPERTS, nb, LANES), F32),
                   jax.ShapeDtypeStruct((N_EXPERTS, nb, 1), F32)],
        compiler_params=pltpu.CompilerParams(
            dimension_semantics=("arbitrary",), vmem_limit_bytes=_vmem_limit(vmem)),
        name="route_compact",
    )(sel.reshape(N_EXPERTS, nb, LANES), a_t.reshape(N_EXPERTS, nb, LANES))
    return (idx.reshape(N_EXPERTS, cap), gate.reshape(N_EXPERTS, cap), pos.reshape(N_EXPERTS, n_tok),
            bex.reshape(N_EXPERTS, nb))


COMBINE_WINDOW = 128
COMBINE_STRIDE = COMBINE_WINDOW - BF16_SUBLANES


def _ple_kernel(lo_ref, cnt_ref, x_ref, pos_ref, p_ref, np_ref, wg_ref, wp_ref, fn_ref, out_hbm, o_ref,
                wbuf, xbuf, y_scr, sem, sem_x, *, final, cap, n_tiles):
    i = pl.program_id(0)
    slot = i % 2
    tm = x_ref.shape[0]
    W = COMBINE_WINDOW
    n_rows = N_EXPERTS * cap

    def win_start(row):
        aligned = jnp.left_shift(jnp.right_shift(row, 4), 4)
        return pl.multiple_of(jnp.minimum(aligned, n_rows - W), BF16_SUBLANES)

    def issue(tile, sl):
        for e in range(N_EXPERTS):
            st = win_start(e * cap + lo_ref[tile * N_EXPERTS + e])
            pltpu.make_async_copy(out_hbm.at[pl.ds(st, W)], wbuf.at[sl, pl.ds(e * W, W)], sem.at[sl]).start()

    @pl.when(i == 0)
    def _():
        issue(0, 0)

    @pl.when(i + 1 < n_tiles)
    def _():
        issue(i + 1, 1 - slot)

    pltpu.make_async_copy(out_hbm.at[pl.ds(0, N_EXPERTS * W)], wbuf.at[slot], sem.at[slot]).wait()

    posv = pos_ref[...]
    lane = lax.broadcasted_iota(jnp.int32, (tm, W), 1).astype(F32)
    parts = []
    for e in range(N_EXPERTS):
        st = win_start(e * cap + lo_ref[i * N_EXPERTS + e])
        rel = posv[:, e:e + 1] - (st - e * cap).astype(F32)
        parts.append((rel == lane).astype(BF16))
    y_scr[...] = jnp.dot(jnp.concatenate(parts, axis=1), wbuf[slot], preferred_element_type=F32)

    n_extra = []
    for e in range(N_EXPERTS):
        first = lo_ref[i * N_EXPERTS + e]
        covered_end = win_start(e * cap + first) - e * cap + W
        remaining = jnp.maximum(first + cnt_ref[i * N_EXPERTS + e] - covered_end, 0)
        n_extra.append((covered_end, (remaining + COMBINE_STRIDE - 1) // COMBINE_STRIDE))

    @pl.when(functools.reduce(lambda a, b: a + b, [n for _, n in n_extra]) > 0)
    def _():
        for e in range(N_EXPERTS):
            covered_end, n_e = n_extra[e]

            def extra(k, carry, e=e, covered_end=covered_end):
                want = covered_end + k * COMBINE_STRIDE
                st = win_start(e * cap + want)
                cp = pltpu.make_async_copy(out_hbm.at[pl.ds(st, W)], xbuf, sem_x)
                cp.start()
                cp.wait()
                pe = pos_ref[:, e:e + 1]
                hit = (pe - (st - e * cap).astype(F32) == lane) & (pe >= want.astype(F32)) \
                    & (pe < (want + COMBINE_STRIDE).astype(F32))
                y_scr[...] += jnp.dot(hit.astype(BF16), xbuf[...], preferred_element_type=F32)
                return carry

            lax.fori_loop(0, n_e, extra, 0)

    x = x_ref[...] + y_scr[...]
    ms = jnp.mean(x * x, axis=-1, keepdims=True)
    h = (x * lax.rsqrt(ms + EPS) * np_ref[...]).astype(BF16)
    gate = _sigmoid(jnp.dot(h, wg_ref[...], preferred_element_type=F32))
    pp = jnp.dot(p_ref[...].astype(BF16), wp_ref[...], preferred_element_type=F32)
    x = x + gate * pp
    if final:
        ms = jnp.mean(x * x, axis=-1, keepdims=True)
        x = x * lax.rsqrt(ms + EPS) * fn_ref[...]
    o_ref[...] = x


def _ple(x2d, out, pos, bex, p_all, layer, norm_ple, wg, wp, final_norm, final):
    T = x2d.shape[0]
    tm = min(TM_PLE, T)
    n_tiles = T // tm
    p_off = layer * n_tiles
    cap = out.shape[0] // N_EXPERTS
    W = COMBINE_WINDOW
    assert cap >= W and cap % BF16_SUBLANES == 0 and tm % LANES == 0
    lo = bex[:, ::tm // LANES].astype(jnp.int32)
    cnt = jnp.diff(lo, axis=1, append=jnp.full((N_EXPERTS, 1), cap, jnp.int32))
    lo = lo.T.reshape(-1)
    cnt = cnt.T.reshape(-1)
    pos_t = pos.T
    vmem = (2 * (2 * tm * D_MODEL * 4 + tm * LANES * 4 + tm * PLE_DIM * 4 + D_MODEL * D_MODEL * 2
                 + PLE_DIM * D_MODEL * 2) + 2 * N_EXPERTS * W * D_MODEL * 2 + W * D_MODEL * 2
            + tm * N_EXPERTS * W * 2 + 6 * tm * D_MODEL * 4)
    tok = lambda width: pl.BlockSpec((tm, width), lambda i, lo, cnt: (i, 0))
    full = lambda shape: pl.BlockSpec(shape, lambda i, lo, cnt: (0,) * len(shape))
    grid_spec = pltpu.PrefetchScalarGridSpec(
        num_scalar_prefetch=2,
        grid=(n_tiles,),
        in_specs=[tok(D_MODEL), tok(N_EXPERTS),
                  pl.BlockSpec((tm, PLE_DIM), lambda i, lo, cnt: (p_off + i, 0)),
                  full((1, D_MODEL)), full((D_MODEL, D_MODEL)),
                  full((PLE_DIM, D_MODEL)), full((1, D_MODEL)), pl.BlockSpec(memory_space=pl.ANY)],
        out_specs=tok(D_MODEL),
        scratch_shapes=[
            pltpu.VMEM((2, N_EXPERTS * W, D_MODEL), BF16),
            pltpu.VMEM((W, D_MODEL), BF16),
            pltpu.VMEM((tm, D_MODEL), F32),
            pltpu.SemaphoreType.DMA((2,)),
            pltpu.SemaphoreType.DMA,
        ],
    )
    return pl.pallas_call(
        functools.partial(_ple_kernel, final=final, cap=cap, n_tiles=n_tiles),
        grid_spec=grid_spec,
        out_shape=jax.ShapeDtypeStruct((T, D_MODEL), F32),
        compiler_params=pltpu.CompilerParams(
            dimension_semantics=("arbitrary",), vmem_limit_bytes=_vmem_limit(vmem)),
        name="combine_ple",
    )(lo, cnt, x2d, pos_t, p_all, norm_ple, wg, wp, final_norm, out)


def _prep_layer(i, norm_mix, w_in, conv_w, na_rpb, gla_w2, gla_b, gla_norm, w_branch, w_out, norm_ffn, w_router,
                w_exp_gate, w_exp_up, w_exp_down, norm_ple, w_ple_gate, w_ple_proj):
    w = w_in[i]
    sizes = (CONV_WIDTH,) * 3 + (NA_WIDTH,) * 3 + (GLA_KEY_WIDTH,) * 2 + (GLA_VAL_WIDTH,) * 2 + (2 * GLA_RANK,)
    offs = np.concatenate([[0], np.cumsum(sizes)])
    gates_off = int(offs[-1])
    lr_off = int(offs[10])
    w_main = jnp.concatenate([w[:, gates_off:], w[:, :lr_off]], axis=1)
    assert float(np.log2(GLA_DK ** -0.5)).is_integer()
    col_scale = np.ones((PROJ_COLS,), np.float32)
    col_scale[COL_NQ:COL_NQ + NA_WIDTH] = NA_HEAD_DIM ** -0.5 * LOG2E
    col_scale[COL_GQ:COL_GQ + GLA_KEY_WIDTH] = GLA_DK ** -0.5
    w_main = (w_main * col_scale).astype(BF16)
    w_lr = jnp.zeros((D_MODEL, LANES), F32).at[:, :2 * GLA_RANK].set(w[:, lr_off:gates_off]).astype(BF16)
    return dict(
        norm_mix=norm_mix[i].reshape(1, D_MODEL), w_main=w_main, w_lr=w_lr, conv_w=conv_w[i], na_rpb=na_rpb[i],
        gla_w2=gla_w2[i], gla_b=gla_b[i], gla_norm=gla_norm[i], w_branch=w_branch[i].astype(BF16),
        w_out=w_out[i].astype(BF16), norm_ffn=norm_ffn[i].reshape(1, D_MODEL), w_router=w_router[i],
        w_exp_gate=w_exp_gate[i].astype(BF16), w_exp_up=w_exp_up[i].astype(BF16),
        w_exp_down=w_exp_down[i].astype(BF16), norm_ple=norm_ple[i].reshape(1, D_MODEL),
        w_ple_gate=w_ple_gate[i].astype(BF16), w_ple_proj=w_ple_proj[i].astype(BF16))


def _moe(hp, aff, lp):
    n_tok = hp.shape[0]
    cap = CAPACITY_FACTOR * n_tok // N_EXPERTS
    idx, gate_val, pos, bex = _route(aff, cap)
    out = _moe_ffn(hp, idx, gate_val, lp["w_exp_gate"], lp["w_exp_up"], lp["w_exp_down"])
    return out, pos, bex


def _trunk(x, p, layers, final_norm):
    bsz, L, _ = x.shape
    T = bsz * L
    x2d = x.reshape(T, D_MODEL)
    fn = final_norm.reshape(1, D_MODEL)
    for i, lp in enumerate(layers):
        proj, lr = _inproj(x2d, lp["norm_mix"], lp["w_main"], lp["w_lr"])
        y_b = _na_attention(proj, lp["na_rpb"], bsz, L)
        y_c = _gla(proj, lr, lp["gla_w2"], lp["gla_b"], lp["gla_norm"], bsz, L)
        x2d, h, aff = _merge(x2d, proj, y_b, y_c, lp["conv_w"], lp["w_branch"], lp["w_out"], lp["norm_ffn"],
                             lp["w_router"], L)
        out, pos, bex = _moe(h, aff, lp)
        x2d = _ple(x2d, out, pos, bex, p.reshape(-1, PLE_DIM), i, lp["norm_ple"], lp["w_ple_gate"], lp["w_ple_proj"], fn,
                   final=(i == len(layers) - 1))
    return x2d.reshape(bsz, L, D_MODEL)


def kernel(x_prompt, x_sample, p_prompt, p_sample, norm_mix, w_in, conv_w, na_rpb, gla_w2, gla_b, gla_norm, w_branch, w_out, norm_ffn, w_router, w_exp_gate, w_exp_up, w_exp_down, norm_ple, w_ple_gate, w_ple_proj, final_norm):
    depth = w_in.shape[0]
    layers = [_prep_layer(i, norm_mix, w_in, conv_w, na_rpb, gla_w2, gla_b, gla_norm, w_branch, w_out, norm_ffn,
                          w_router, w_exp_gate, w_exp_up, w_exp_down, norm_ple, w_ple_gate, w_ple_proj)
              for i in range(depth)]
    y_prompt = _trunk(x_prompt, p_prompt, layers, final_norm)
    y_sample = _trunk(x_sample, p_sample, layers, final_norm)
    return (y_prompt, y_sample)
```

```python
import functools

import numpy as np
import jax
import jax.numpy as jnp
from jax import lax
from jax.experimental import pallas as pl
from jax.experimental.pallas import tpu as pltpu

F32 = jnp.float32
BF16 = jnp.bfloat16

D_MODEL = 1024
GRID_W = 64
PLE_DIM = 256
EPS = 1e-6
CONV_WIDTH = 512
CONV_K = 3
NA_HEADS = 8
NA_HEAD_DIM = 64
NA_WIDTH = NA_HEADS * NA_HEAD_DIM
WIN_ROWS = 8
WIN_COLS = 16
GLA_HEADS = 4
GLA_DK = 64
GLA_DV = 128
GLA_KEY_WIDTH = GLA_HEADS * GLA_DK
GLA_VAL_WIDTH = GLA_HEADS * GLA_DV
GLA_RANK = 16
GLA_GATE_NORM = 16.0
GLA_CHUNK = 64
N_BRANCH = 3
N_EXPERTS = 16
EXPERT_FF = 2048
CAPACITY_FACTOR = 2

V7X_VMEM_BYTES = 64 * 1024 * 1024
LANES = 128
BF16_SUBLANES = 16

PROJ_TN = 512
COL_GATES = 0
COL_AB = 3 * D_MODEL
COL_AC = COL_AB + CONV_WIDTH
COL_AX = COL_AC + CONV_WIDTH
COL_NQ = COL_AX + CONV_WIDTH
COL_NK = COL_NQ + NA_WIDTH
COL_NV = COL_NK + NA_WIDTH
COL_GQ = COL_NV + NA_WIDTH
COL_GK = COL_GQ + GLA_KEY_WIDTH
COL_GV = COL_GK + GLA_KEY_WIDTH
COL_GG = COL_GV + GLA_VAL_WIDTH
PROJ_COLS = COL_GG + GLA_VAL_WIDTH
N_GATE_BLOCKS = (3 * D_MODEL) // PROJ_TN

TM_INPROJ = 512
NA_TILE_ROWS = 4
NA_TQ = NA_TILE_ROWS * GRID_W
GLA_TC = 512
TM_MERGE = 256
TM_FFN = 512
TM_PLE = 512

NEG_INF = -1e30
LOG2E = float(np.log2(np.e))


def _vmem_limit(nbytes):
    return int(min(V7X_VMEM_BYTES - 6 * 1024 * 1024, max(32 * 1024 * 1024, nbytes + 8 * 1024 * 1024)))


def _sigmoid(x):
    return 1.0 / (1.0 + jnp.exp(-x))


def _split3(a):
    hi = a.astype(BF16)
    r1 = a - hi.astype(F32)
    mid = r1.astype(BF16)
    lo = (r1 - mid.astype(F32)).astype(BF16)
    return hi, mid, lo


def _inproj_kernel(x_ref, nw_ref, wlr_ref, w_hbm, proj_ref, lr_ref, w_vmem, sem):
    @pl.when(pl.program_id(0) == 0)
    def _():
        cp = pltpu.make_async_copy(w_hbm, w_vmem, sem)
        cp.start()
        cp.wait()

    x = x_ref[...]
    ms = jnp.mean(x * x, axis=-1, keepdims=True)
    hb = (x * lax.rsqrt(ms + EPS) * nw_ref[...]).astype(BF16)
    lr_ref[...] = jnp.dot(hb, wlr_ref[...], preferred_element_type=F32)
    for j in range(PROJ_COLS // PROJ_TN):
        cs = slice(j * PROJ_TN, (j + 1) * PROJ_TN)
        acc = jnp.dot(hb, w_vmem[:, cs], preferred_element_type=F32)
        if j < N_GATE_BLOCKS:
            acc = _sigmoid(acc)
        proj_ref[:, cs] = acc.astype(BF16)


def _inproj(x2d, norm_w, w_main, w_lr):
    T = x2d.shape[0]
    tm = min(TM_INPROJ, T)
    vmem = (2 * (tm * D_MODEL * 4 + D_MODEL * LANES * 2 + tm * PROJ_COLS * 2 + tm * LANES * 4)
            + D_MODEL * PROJ_COLS * 2 + tm * D_MODEL * 2 + 4 * tm * PROJ_TN * 4)
    return pl.pallas_call(
        _inproj_kernel,
        grid=(T // tm,),
        in_specs=[
            pl.BlockSpec((tm, D_MODEL), lambda i: (i, 0)),
            pl.BlockSpec((1, D_MODEL), lambda i: (0, 0)),
            pl.BlockSpec((D_MODEL, LANES), lambda i: (0, 0)),
            pl.BlockSpec(memory_space=pl.ANY),
        ],
        out_specs=[
            pl.BlockSpec((tm, PROJ_COLS), lambda i: (i, 0)),
            pl.BlockSpec((tm, LANES), lambda i: (i, 0)),
        ],
        out_shape=[
            jax.ShapeDtypeStruct((T, PROJ_COLS), BF16),
            jax.ShapeDtypeStruct((T, LANES), F32),
        ],
        scratch_shapes=[pltpu.VMEM((D_MODEL, PROJ_COLS), BF16), pltpu.SemaphoreType.DMA],
        compiler_params=pltpu.CompilerParams(
            dimension_semantics=("arbitrary",), vmem_limit_bytes=_vmem_limit(vmem)),
        name="inproj",
    )(x2d, norm_w, w_lr, w_main)


def _na_tables(rows):
    R = NA_TILE_ROWS
    J = rows // R
    kr = min(WIN_ROWS, rows)
    qr_rel = np.arange(NA_TQ) // GRID_W
    qc = np.arange(NA_TQ) % GRID_W
    kk = np.arange(3 * NA_TQ)
    kblk = kk // NA_TQ
    kwithin = kk % NA_TQ
    kc = kwithin % GRID_W
    col_start = np.clip(qc - WIN_COLS // 2, 0, GRID_W - WIN_COLS)
    col_ok = (kc[None, :] >= col_start[:, None]) & (kc[None, :] < col_start[:, None] + WIN_COLS)
    dc = np.clip(kc[None, :] - qc[:, None] + WIN_COLS - 1, 0, 2 * WIN_COLS - 2)
    types, keys, tile_type = [], {}, []
    for j in range(J):
        blocks = np.array([max(j - 1, 0), j, min(j + 1, J - 1)])
        blk_ok = np.array([j - 1 >= 0, True, j + 1 <= J - 1])
        krow = blocks[kblk] * R + kwithin // GRID_W
        qrow = j * R + qr_rel
        rs = np.clip(qrow - kr // 2, 0, rows - kr)
        row_ok = (krow[None, :] >= rs[:, None]) & (krow[None, :] < rs[:, None] + kr) & blk_ok[kblk][None, :]
        mask = row_ok & col_ok
        dr = np.clip(krow[None, :] - qrow[:, None] + WIN_ROWS - 1, 0, 2 * WIN_ROWS - 2)
        dr = np.where(mask, dr, 0)
        key = mask.tobytes() + dr.astype(np.int8).tobytes()
        if key not in keys:
            keys[key] = len(types)
            types.append((mask, dr, np.where(mask, dc, 0)))
        tile_type.append(keys[key])
    mask = np.stack([t[0] for t in types])
    dr = np.stack([t[1] for t in types]).astype(np.int32)
    dcs = np.stack([t[2] for t in types]).astype(np.int32)
    return mask, dr, dcs, np.array(tile_type, np.int32)


def _na_bias(rpb, mask, dr):
    n_types = mask.shape[0]
    W = GRID_W
    qc = np.arange(W)[:, None]
    kc = np.arange(W)[None, :]
    col_start = np.clip(qc - WIN_COLS // 2, 0, W - WIN_COLS)
    col_ok = (kc >= col_start) & (kc < col_start + WIN_COLS)
    n_dc = 2 * WIN_COLS - 1
    lead = W - WIN_COLS
    padded = jnp.pad(rpb, ((0, 0), (0, 0), (lead, 2 * W - 1 - lead - n_dc)), constant_values=NEG_INF)
    toep = jnp.stack([padded[:, :, W - 1 - q:2 * W - 1 - q] for q in range(W)], axis=2)
    toep = jnp.where(col_ok[None, None], toep, NEG_INF)
    dead = jnp.full((NA_HEADS, 1, W, W), NEG_INF, F32)
    blocks = jnp.concatenate([toep, dead], axis=1)
    n_dead = toep.shape[1]
    tables = []
    for t in range(n_types):
        ids = np.where(mask[t, ::W, ::W], dr[t, ::W, ::W], n_dead)
        rows_ = [jnp.concatenate([blocks[:, int(b)] for b in ids_row], axis=-1) for ids_row in ids]
        tables.append(jnp.concatenate(rows_, axis=1))
    return jnp.stack(tables)


def _na_kernel(tid_ref, q_ref, kp_ref, kc_ref, kn_ref, vp_ref, vc_ref, vn_ref, bias_ref, o_ref):
    del tid_ref
    q = q_ref[...]
    k = jnp.concatenate([kp_ref[...], kc_ref[...], kn_ref[...]], axis=0)
    v = jnp.concatenate([vp_ref[...], vc_ref[...], vn_ref[...]], axis=0)
    outs = []
    for h in range(NA_HEADS):
        sl = slice(h * NA_HEAD_DIM, (h + 1) * NA_HEAD_DIM)
        s = lax.dot_general(q[:, sl], k[:, sl], (((1,), (1,)), ((), ())), preferred_element_type=F32)
        s = s + bias_ref[0, h]
        m = jnp.max(s, axis=-1, keepdims=True)
        p = jnp.exp2(s - m)
        l = jnp.sum(p, axis=-1, keepdims=True)
        o = jnp.dot(p.astype(BF16), v[:, sl], preferred_element_type=F32)
        outs.append(o * (1.0 / l))
    o_ref[...] = jnp.concatenate(outs, axis=-1).astype(BF16)


def _na_attention(proj, rpb, bsz, L):
    T = proj.shape[0]
    rows = L // GRID_W
    J = rows // NA_TILE_ROWS
    mask, dr, _, tile_type = _na_tables(rows)
    bias = _na_bias(rpb.astype(F32) * LOG2E, mask, dr)
    qb, kb, vb = COL_NQ // NA_WIDTH, COL_NK // NA_WIDTH, COL_NV // NA_WIDTH

    def prev(b, j, tid):
        return b * J + jnp.maximum(j - 1, 0)

    def nxt(b, j, tid):
        return b * J + jnp.minimum(j + 1, J - 1)

    blk = (NA_TQ, NA_WIDTH)
    vmem = 2 * (8 * NA_TQ * NA_WIDTH * 2 + NA_HEADS * NA_TQ * 3 * NA_TQ * 4) + 8 * NA_TQ * 3 * NA_TQ * 4
    grid_spec = pltpu.PrefetchScalarGridSpec(
        num_scalar_prefetch=1,
        grid=(bsz, J),
        in_specs=[
            pl.BlockSpec(blk, lambda b, j, tid: (b * J + j, qb)),
            pl.BlockSpec(blk, lambda b, j, tid: (prev(b, j, tid), kb)),
            pl.BlockSpec(blk, lambda b, j, tid: (b * J + j, kb)),
            pl.BlockSpec(blk, lambda b, j, tid: (nxt(b, j, tid), kb)),
            pl.BlockSpec(blk, lambda b, j, tid: (prev(b, j, tid), vb)),
            pl.BlockSpec(blk, lambda b, j, tid: (b * J + j, vb)),
            pl.BlockSpec(blk, lambda b, j, tid: (nxt(b, j, tid), vb)),
            pl.BlockSpec((1, NA_HEADS, NA_TQ, 3 * NA_TQ), lambda b, j, tid: (tid[j], 0, 0, 0)),
        ],
        out_specs=pl.BlockSpec(blk, lambda b, j, tid: (b * J + j, 0)),
    )
    return pl.pallas_call(
        _na_kernel,
        grid_spec=grid_spec,
        out_shape=jax.ShapeDtypeStruct((T, NA_WIDTH), BF16),
        compiler_params=pltpu.CompilerParams(
            dimension_semantics=("arbitrary", "arbitrary"), vmem_limit_bytes=_vmem_limit(vmem)),
        name="na_attention",
    )(jnp.asarray(tile_type), proj, proj, proj, proj, proj, proj, proj, bias)


def _gla_kernel(q_ref, k_ref, v_ref, gg_ref, lr_ref, cum_ref, w2_ref, gb_ref, gn_ref, o_ref, ofwd_scr, st_scr,
                *, nt):
    p = pl.program_id(1)
    j = pl.program_id(2)
    c = GLA_CHUNK
    n_chunks = GLA_TC // c
    H = GLA_HEADS

    @pl.when(j == 0)
    def _():
        st_scr[...] = jnp.zeros_like(st_scr)

    lane_head = lax.broadcasted_iota(jnp.int32, (1, GLA_KEY_WIDTH), 1) // GLA_DK
    ri = lax.broadcasted_iota(jnp.int32, (H * c, c), 0) % c
    ci = lax.broadcasted_iota(jnp.int32, (H * c, c), 1)

    def sweep(direction):
        tri = (ci <= ri) if direction == 0 else (ci >= ri)
        z = lr_ref[...].astype(BF16)
        xg = jnp.dot(z, w2_ref[direction], preferred_element_type=F32) + gb_ref[direction]
        la = (jnp.minimum(xg, 0.0) - jnp.log1p(jnp.exp(-jnp.abs(xg)))) * (1.0 / GLA_GATE_NORM)
        hi, mid, lo = _split3(la)
        cum = cum_ref[direction]
        b_all = (jnp.dot(cum, hi, preferred_element_type=F32)
                 + jnp.dot(cum, mid, preferred_element_type=F32)
                 + jnp.dot(cum, lo, preferred_element_type=F32))
        k_all = k_ref[...].astype(F32)
        qd_all = q_ref[...].astype(F32) * jnp.exp(b_all)
        kd_all = (k_all * jnp.exp(-b_all)).astype(BF16)
        st = st_scr[...]
        order = range(n_chunks) if direction == 0 else range(n_chunks - 1, -1, -1)
        for ch in order:
            rs = slice(ch * c, (ch + 1) * c)
            b = b_all[rs]
            b_last = b[c - 1:c, :] if direction == 0 else b[0:1, :]
            v = v_ref[rs, :]
            qd = qd_all[rs]
            kd = kd_all[rs]
            krem = (k_all[rs] * jnp.exp(b_last - b)).astype(BF16)
            st_b = st.astype(BF16)
            nt_dims = (((1,), (1,)), ((), ()))
            qs = jnp.concatenate([jnp.where(lane_head == h, qd, 0.0) for h in range(H)], axis=0).astype(BF16)
            att = lax.dot_general(qs, kd, nt_dims, preferred_element_type=F32)
            att = jnp.where(tri, att, 0.0).astype(BF16)
            intra = jnp.dot(att, v, preferred_element_type=F32)
            inter = lax.dot_general(qs, st_b, nt_dims, preferred_element_type=F32)
            o_chunk = jnp.concatenate(
                [intra[h * c:(h + 1) * c, h * GLA_DV:(h + 1) * GLA_DV] + inter[h * c:(h + 1) * c]
                 for h in range(H)], axis=-1)
            kv = lax.dot_general(v, krem, (((0,), (0,)), ((), ())), preferred_element_type=F32)
            st = st * jnp.exp(b_last)
            for h in range(H):
                st = st + jnp.where(lane_head == h, kv[h * GLA_DV:(h + 1) * GLA_DV, :], 0.0)
            yield ch, o_chunk
        st_scr[...] = st

    @pl.when(p == 0)
    def _():
        for ch, o_chunk in sweep(0):
            start = pl.multiple_of(j * GLA_TC + ch * c, c)
            ofwd_scr[pl.ds(start, c), :] = o_chunk

    @pl.when(p == 1)
    def _():
        t = nt - 1 - j
        gn = gn_ref[...]
        for ch, o_chunk in sweep(1):
            start = pl.multiple_of(t * GLA_TC + ch * c, c)
            o = ofwd_scr[pl.ds(start, c), :] + o_chunk
            normed = []
            for h in range(H):
                oh = o[:, h * GLA_DV:(h + 1) * GLA_DV]
                ms = jnp.mean(oh * oh, axis=-1, keepdims=True)
                normed.append(oh * lax.rsqrt(ms + EPS))
            on = jnp.concatenate(normed, axis=-1) * gn
            g = gg_ref[ch * c:(ch + 1) * c, :].astype(F32)
            o_ref[ch * c:(ch + 1) * c, :] = (g * _sigmoid(g) * on).astype(BF16)


def _gla(proj, lr, gla_w2, gla_b, gla_norm, bsz, L):
    T = proj.shape[0]
    nt = L // GLA_TC
    w2p = jnp.zeros((2, LANES, GLA_KEY_WIDTH), F32)
    w2p = w2p.at[0, 0:GLA_RANK].set(gla_w2[0]).at[1, GLA_RANK:2 * GLA_RANK].set(gla_w2[1]).astype(BF16)
    gb = gla_b.reshape(2, 1, GLA_KEY_WIDTH).astype(F32)
    gn = gla_norm.reshape(1, GLA_VAL_WIDTH).astype(F32)
    r = np.arange(GLA_TC)
    same = (r[:, None] // GLA_CHUNK) == (r[None, :] // GLA_CHUNK)
    cum = jnp.asarray(np.stack([same & (r[None, :] <= r[:, None]), same & (r[None, :] >= r[:, None])]), BF16)

    def tile(b, p, j):
        return b * nt + jnp.where(p == 0, j, nt - 1 - j)

    def otile(b, p, j):
        return b * nt + jnp.where(p == 0, nt - 1, nt - 1 - j)

    kq, kk = COL_GQ // GLA_KEY_WIDTH, COL_GK // GLA_KEY_WIDTH
    kv, kg = COL_GV // GLA_VAL_WIDTH, COL_GG // GLA_VAL_WIDTH
    vmem = (2 * (2 * GLA_TC * GLA_KEY_WIDTH * 2 + 3 * GLA_TC * GLA_VAL_WIDTH * 2 + GLA_TC * LANES * 4
                 + 2 * GLA_TC * GLA_TC * 2)
            + L * GLA_VAL_WIDTH * 4 + GLA_DV * GLA_KEY_WIDTH * 4 + 16 * GLA_TC * GLA_KEY_WIDTH * 4)
    return pl.pallas_call(
        functools.partial(_gla_kernel, nt=nt),
        grid=(bsz, 2, nt),
        in_specs=[
            pl.BlockSpec((GLA_TC, GLA_KEY_WIDTH), lambda b, p, j: (tile(b, p, j), kq)),
            pl.BlockSpec((GLA_TC, GLA_KEY_WIDTH), lambda b, p, j: (tile(b, p, j), kk)),
            pl.BlockSpec((GLA_TC, GLA_VAL_WIDTH), lambda b, p, j: (tile(b, p, j), kv)),
            pl.BlockSpec((GLA_TC, GLA_VAL_WIDTH), lambda b, p, j: (tile(b, p, j), kg)),
            pl.BlockSpec((GLA_TC, LANES), lambda b, p, j: (tile(b, p, j), 0)),
            pl.BlockSpec((2, GLA_TC, GLA_TC), lambda b, p, j: (0, 0, 0)),
            pl.BlockSpec((2, LANES, GLA_KEY_WIDTH), lambda b, p, j: (0, 0, 0)),
            pl.BlockSpec((2, 1, GLA_KEY_WIDTH), lambda b, p, j: (0, 0, 0)),
            pl.BlockSpec((1, GLA_VAL_WIDTH), lambda b, p, j: (0, 0)),
        ],
        out_specs=pl.BlockSpec((GLA_TC, GLA_VAL_WIDTH), lambda b, p, j: (otile(b, p, j), 0)),
        out_shape=jax.ShapeDtypeStruct((T, GLA_VAL_WIDTH), BF16),
        scratch_shapes=[pltpu.VMEM((L, GLA_VAL_WIDTH), F32), pltpu.VMEM((GLA_DV, GLA_KEY_WIDTH), F32)],
        compiler_params=pltpu.CompilerParams(
            dimension_semantics=("arbitrary", "arbitrary", "arbitrary"), vmem_limit_bytes=_vmem_limit(vmem)),
        name="gla",
    )(proj, proj, proj, proj, lr, cum, w2p, gb, gn)


def _merge_kernel(x_ref, ga_ref, gb_ref, gc_ref, ab_ref, ac_ref, ax_ref, acp_ref, axp_ref, acn_ref, axn_ref,
                  yb_ref, yc_ref, cw_ref, wb_ref, wo_ref, nf_ref, wr_ref, xo_ref, h_ref, aff_ref, *, tiles_per_seq):
    i = pl.program_id(0)
    tm = x_ref.shape[0]
    pos = i % tiles_per_seq
    has_prev = jnp.where(pos == 0, 0.0, 1.0)
    has_next = jnp.where(pos == tiles_per_seq - 1, 0.0, 1.0)
    u = ac_ref[...].astype(F32) * ax_ref[...].astype(F32)
    up = (acp_ref[...].astype(F32) * axp_ref[...].astype(F32))[BF16_SUBLANES - 1:BF16_SUBLANES, :] * has_prev
    un = (acn_ref[...].astype(F32) * axn_ref[...].astype(F32))[0:1, :] * has_next
    ridx = lax.broadcasted_iota(jnp.int32, (tm, 1), 0)
    u_prev = jnp.where(ridx == 0, up, pltpu.roll(u, 1, axis=0))
    u_next = jnp.where(ridx == tm - 1, un, pltpu.roll(u, tm - 1, axis=0))
    cw = cw_ref[...]
    y_a = ab_ref[...].astype(F32) * (u_prev * cw[0:1, :] + u * cw[1:2, :] + u_next * cw[2:3, :])
    m = ga_ref[...].astype(F32) * jnp.dot(y_a.astype(BF16), wb_ref[0], preferred_element_type=F32)
    m = m + gb_ref[...].astype(F32) * jnp.dot(yb_ref[...], wb_ref[1], preferred_element_type=F32)
    m = m + gc_ref[...].astype(F32) * jnp.dot(yc_ref[...], wb_ref[2], preferred_element_type=F32)
    x_new = x_ref[...] + jnp.dot(m.astype(BF16), wo_ref[...], preferred_element_type=F32)
    xo_ref[...] = x_new
    ms = jnp.mean(x_new * x_new, axis=-1, keepdims=True)
    h = x_new * lax.rsqrt(ms + EPS) * nf_ref[...]
    bits = lax.bitcast_convert_type(h.astype(BF16).astype(F32), jnp.uint32)
    half = D_MODEL // 2
    h_ref[...] = (bits[:, :half] >> 16) | (bits[:, half:] & jnp.uint32(0xFFFF0000))
    h0, h1, _ = _split3(h)
    w0, w1 = wr_ref[0], wr_ref[1]
    logits = (jnp.dot(h0, w0, preferred_element_type=F32) + jnp.dot(h0, w1, preferred_element_type=F32)
              + jnp.dot(h1, w0, preferred_element_type=F32))
    lane = lax.broadcasted_iota(jnp.int32, logits.shape, 1)
    logits = jnp.where(lane < N_EXPERTS, logits, NEG_INF)
    mx = jnp.max(logits, axis=-1, keepdims=True)
    e = jnp.exp(logits - mx)
    aff_ref[...] = e / jnp.sum(e, axis=-1, keepdims=True)


def _merge(x2d, proj, y_b, y_c, conv_w, wb, wo, norm_ffn, w_router, L):
    T = x2d.shape[0]
    tm = min(TM_MERGE, L)
    tiles_per_seq = L // tm
    n16 = T // BF16_SUBLANES
    per16 = tm // BF16_SUBLANES
    wr = jnp.zeros((D_MODEL, LANES), F32).at[:, :N_EXPERTS].set(w_router)
    wr = jnp.stack(_split3(wr)[:2])
    cb = CONV_WIDTH
    vmem = 2 * (2 * tm * D_MODEL * 4 + 3 * tm * D_MODEL * 2 + 5 * tm * cb * 2 + 4 * BF16_SUBLANES * cb * 2
                + 3 * cb * D_MODEL * 2 + D_MODEL * D_MODEL * 2 + 3 * D_MODEL * LANES * 2
                + tm * D_MODEL * 2 + tm * LANES * 4) + 8 * tm * D_MODEL * 4
    tok = lambda width, col: pl.BlockSpec((tm, width), lambda i: (i, col // width))
    halo_p = lambda col: pl.BlockSpec((BF16_SUBLANES, cb), lambda i: (jnp.maximum(i * per16 - 1, 0), col // cb))
    halo_n = lambda col: pl.BlockSpec((BF16_SUBLANES, cb),
                                      lambda i: (jnp.minimum((i + 1) * per16, n16 - 1), col // cb))
    full = lambda shape: pl.BlockSpec(shape, lambda i: (0,) * len(shape))
    return pl.pallas_call(
        functools.partial(_merge_kernel, tiles_per_seq=tiles_per_seq),
        grid=(T // tm,),
        in_specs=[
            tok(D_MODEL, 0),
            tok(D_MODEL, COL_GATES), tok(D_MODEL, COL_GATES + D_MODEL), tok(D_MODEL, COL_GATES + 2 * D_MODEL),
            tok(cb, COL_AB), tok(cb, COL_AC), tok(cb, COL_AX),
            halo_p(COL_AC), halo_p(COL_AX), halo_n(COL_AC), halo_n(COL_AX),
            tok(NA_WIDTH, 0), tok(GLA_VAL_WIDTH, 0),
            full((CONV_K, cb)), full((N_BRANCH, cb, D_MODEL)), full((D_MODEL, D_MODEL)),
            full((1, D_MODEL)), full((2, D_MODEL, LANES)),
        ],
        out_specs=[tok(D_MODEL, 0), tok(D_MODEL // 2, 0), tok(LANES, 0)],
        out_shape=[
            jax.ShapeDtypeStruct((T, D_MODEL), F32),
            jax.ShapeDtypeStruct((T, D_MODEL // 2), jnp.uint32),
            jax.ShapeDtypeStruct((T, LANES), F32),
        ],
        compiler_params=pltpu.CompilerParams(
            dimension_semantics=("arbitrary",), vmem_limit_bytes=_vmem_limit(vmem)),
        name="merge",
    )(x2d, proj, proj, proj, proj, proj, proj, proj, proj, proj, proj, y_b, y_c,
      conv_w, wb, wo, norm_ffn, wr)


DMA_UNROLL = 8


def _moe_kernel(idx_ref, idxn_ref, g_ref, wg_ref, wu_ref, wd_ref, hp_hbm, o_ref, hbuf, sem_h, *, n_steps):
    tm = hbuf.shape[1]
    s = pl.program_id(0)
    slot = s % 2

    def gather_rows(idx, sl):
        def body(g, c):
            for k in range(DMA_UNROLL):
                r = g * DMA_UNROLL + k
                t = idx[0, 0, r]
                pltpu.make_async_copy(hp_hbm.at[pl.ds(t, 1)], hbuf.at[sl, pl.ds(r, 1)],
                                      sem_h.at[sl]).start(priority=k % 2)
            return c
        lax.fori_loop(0, tm // DMA_UNROLL, body, 0)

    @pl.when(s == 0)
    def _():
        gather_rows(idx_ref, slot)

    def wait_rows(sl):
        pltpu.make_async_copy(hp_hbm.at[pl.ds(0, tm)], hbuf.at[sl], sem_h.at[sl]).wait()

    def prefetch_quarter(qtr):
        for r in range(qtr * (tm // 4), (qtr + 1) * (tm // 4)):
            t = idxn_ref[0, 0, r]
            pltpu.make_async_copy(hp_hbm.at[pl.ds(t, 1)], hbuf.at[1 - slot, pl.ds(r, 1)],
                                  sem_h.at[1 - slot]).start(priority=r % 2)

    wait_rows(slot)
    w = hbuf[slot]
    lo = lax.bitcast_convert_type(w << 16, F32)
    hi = lax.bitcast_convert_type(w & jnp.uint32(0xFFFF0000), F32)
    xe = jnp.concatenate([lo, hi], axis=1).astype(BF16)
    half = EXPERT_FF // 2
    acc = None
    for f in range(2):
        fs = slice(f * half, (f + 1) * half)
        a = jnp.dot(xe, wg_ref[0, :, fs], preferred_element_type=F32)
        prefetch_quarter(2 * f)
        b = jnp.dot(xe, wu_ref[0, :, fs], preferred_element_type=F32)
        he = (a * _sigmoid(a) * b).astype(BF16)
        prefetch_quarter(2 * f + 1)
        part = jnp.dot(he, wd_ref[0, fs, :], preferred_element_type=F32)
        acc = part if acc is None else acc + part
    o_ref[...] = (acc * g_ref[0]).astype(BF16)

    @pl.when(s == n_steps - 1)
    def _():
        wait_rows(1 - slot)


def _moe_ffn(hp, idx, gate, wg, wu, wd):
    E, cap = idx.shape
    tm = min(TM_FFN, cap)
    assert tm % DMA_UNROLL == 0
    nt = cap // tm
    n_steps = E * nt
    half = D_MODEL // 2
    idx3 = idx.reshape(n_steps, 1, tm).astype(jnp.int32)
    g3 = gate.reshape(n_steps, tm, 1).astype(F32)
    vmem = (2 * (3 * D_MODEL * EXPERT_FF * 2 + tm * LANES * 4 + tm * D_MODEL * 2) + 2 * tm * half * 4
            + 3 * tm * EXPERT_FF * 4)
    smem = lambda f: pl.BlockSpec((1, 1, tm), f, memory_space=pltpu.SMEM)
    wspec = lambda shape: pl.BlockSpec((1,) + shape, lambda s: (s // nt, 0, 0))
    return pl.pallas_call(
        functools.partial(_moe_kernel, n_steps=n_steps),
        grid=(n_steps,),
        in_specs=[
            smem(lambda s: (s, 0, 0)),
            smem(lambda s: (jnp.minimum(s + 1, n_steps - 1), 0, 0)),
            pl.BlockSpec((1, tm, 1), lambda s: (s, 0, 0)),
            wspec((D_MODEL, EXPERT_FF)), wspec((D_MODEL, EXPERT_FF)), wspec((EXPERT_FF, D_MODEL)),
            pl.BlockSpec(memory_space=pl.ANY),
        ],
        out_specs=pl.BlockSpec((tm, D_MODEL), lambda s: (s, 0)),
        out_shape=jax.ShapeDtypeStruct((E * cap, D_MODEL), BF16),
        scratch_shapes=[pltpu.VMEM((2, tm, half), jnp.uint32), pltpu.SemaphoreType.DMA((2,))],
        compiler_params=pltpu.CompilerParams(
            dimension_semantics=("arbitrary",), vmem_limit_bytes=_vmem_limit(vmem)),
        name="moe_ffn",
    )(idx3, idx3, g3, wg, wu, wd, hp)


ROUTE_SLOT_CHUNK = 1024


def _select_kernel(a_ref, sel_ref, *, cap):
    keys = lax.bitcast_convert_type(a_ref[...], jnp.int32)
    n_exp, n_tok = keys.shape
    capf = jnp.float32(cap)

    def count(pred):
        return jnp.sum(pred.astype(F32), axis=1, keepdims=True)

    def value_bit(b, lo):
        cand = lo | jnp.left_shift(jnp.int32(1), 30 - b)
        return jnp.where(count(keys >= cand) >= capf, cand, lo)

    thr = lax.fori_loop(0, 31, value_bit, jnp.zeros((n_exp, 1), jnp.int32))
    gt = keys > thr
    eq = keys == thr
    need = capf - count(gt)
    tok = lax.broadcasted_iota(jnp.int32, keys.shape, 1)
    nbits = int(n_tok).bit_length()

    def index_bit(b, m):
        cand = m | jnp.left_shift(jnp.int32(1), nbits - 1 - b)
        ok = (cand <= n_tok) & (count(eq & (tok < cand)) <= need)
        return jnp.where(ok, cand, m)

    m = lax.fori_loop(0, nbits, index_bit, jnp.zeros((n_exp, 1), jnp.int32))
    sel_ref[...] = (gt | (eq & (tok < m))).astype(F32)


def _compact_kernel(sel_ref, aff_ref, idx_ref, gate_ref, pos_ref, bex_ref, *, cap, n_tok):
    sel = sel_ref[0]
    nb = sel.shape[0]
    ch = min(ROUTE_SLOT_CHUNK, cap)
    nt_dims = (((1,), (1,)), ((), ()))
    r128 = lax.broadcasted_iota(jnp.int32, (LANES, LANES), 0)
    c128 = lax.broadcasted_iota(jnp.int32, (LANES, LANES), 1)
    rj = lax.broadcasted_iota(jnp.int32, (nb, nb), 0)
    cj = lax.broadcasted_iota(jnp.int32, (nb, nb), 1)
    selb = sel.astype(BF16)
    cin = jnp.dot(selb, (r128 <= c128).astype(BF16), preferred_element_type=F32)
    totb = jnp.broadcast_to(cin[:, LANES - 1:LANES], (nb, LANES)).astype(BF16)
    bex = jnp.dot((cj < rj).astype(BF16), totb, preferred_element_type=F32)
    tot_row = lax.dot_general(jnp.ones((8, LANES), BF16), selb, nt_dims, preferred_element_type=F32)
    binc_row = jnp.dot(tot_row.astype(BF16), (rj <= cj).astype(BF16), preferred_element_type=F32)[0:1]
    pos_ref[0] = jnp.where(sel > 0.5, bex + cin - 1.0, -1.0)
    bex_ref[0] = bex[:, 0:1]
    bex_hi = jnp.floor(bex * (1.0 / 256.0))
    bex_lo = bex - 256.0 * bex_hi
    a_hi, a_mid, a_lo = _split3(aff_ref[0])
    table = jnp.concatenate([cin.astype(BF16), bex_hi.astype(BF16), bex_lo.astype(BF16), a_hi, a_mid, a_lo], axis=1)
    lane_nb = lax.broadcasted_iota(jnp.int32, (ch, nb), 1).astype(F32)
    lane128 = lax.broadcasted_iota(jnp.int32, (ch, LANES), 1).astype(F32)
    for c in range(cap // ch):
        s = (lax.broadcasted_iota(jnp.int32, (ch, 1), 0) + c * ch).astype(F32)
        blk = jnp.sum((binc_row <= s).astype(F32), axis=1, keepdims=True)
        onehot = (lane_nb == blk).astype(BF16)
        g = jnp.dot(onehot, table, preferred_element_type=F32)
        cinrow = g[:, 0:LANES]
        before = g[:, LANES:2 * LANES] * 256.0 + g[:, 2 * LANES:3 * LANES]
        affrow = g[:, 3 * LANES:4 * LANES] + g[:, 4 * LANES:5 * LANES] + g[:, 5 * LANES:6 * LANES]
        off = jnp.sum((cinrow <= s - before).astype(F32), axis=1, keepdims=True)
        idx = jnp.minimum(blk * float(LANES) + off, float(n_tok - 1))
        gate = jnp.sum(jnp.where(lane128 == off, affrow, 0.0), axis=1, keepdims=True)
        idx_t = jnp.transpose(jnp.broadcast_to(idx, (ch, LANES)))[0:1]
        gate_t = jnp.transpose(jnp.broadcast_to(gate, (ch, LANES)))[0:1]
        idx_ref[0, :, c * ch:(c + 1) * ch] = idx_t.astype(jnp.int32)
        gate_ref[0, :, c * ch:(c + 1) * ch] = gate_t


def _route(aff, cap):
    n_tok = aff.shape[0]
    nb = n_tok // LANES
    a_t = aff[:, :N_EXPERTS].T
    sel = pl.pallas_call(
        functools.partial(_select_kernel, cap=cap),
        out_shape=jax.ShapeDtypeStruct((N_EXPERTS, n_tok), F32),
        compiler_params=pltpu.CompilerParams(vmem_limit_bytes=_vmem_limit(12 * N_EXPERTS * n_tok * 4)),
        name="route_select",
    )(a_t)
    ch = min(ROUTE_SLOT_CHUNK, cap)
    vmem = 2 * (2 * nb * LANES * 4 + 2 * cap * 4) + 2 * nb * nb * 4 + nb * 6 * LANES * 4 + ch * (nb + 12 * LANES) * 4
    blk = pl.BlockSpec((1, nb, LANES), lambda e: (e, 0, 0))
    oblk = pl.BlockSpec((1, 1, cap), lambda e: (e, 0, 0))
    idx, gate, pos, bex = pl.pallas_call(
        functools.partial(_compact_kernel, cap=cap, n_tok=n_tok),
        grid=(N_EXPERTS,),
        in_specs=[blk, blk],
        out_specs=[oblk, oblk, blk, pl.BlockSpec((1, nb, 1), lambda e: (e, 0, 0))],
        out_shape=[jax.ShapeDtypeStruct((N_EXPERTS, 1, cap), jnp.int32),
                   jax.ShapeDtypeStruct((N_EXPERTS, 1, cap), F32),
                   jax.ShapeDtypeStruct((N_EXPERTS, nb, LANES), F32),
                   jax.ShapeDtypeStruct((N_EXPERTS, nb, 1), F32)],
        compiler_params=pltpu.CompilerParams(
            dimension_semantics=("arbitrary",), vmem_limit_bytes=_vmem_limit(vmem)),
        name="route_compact",
    )(sel.reshape(N_EXPERTS, nb, LANES), a_t.reshape(N_EXPERTS, nb, LANES))
    return (idx.reshape(N_EXPERTS, cap), gate.reshape(N_EXPERTS, cap), pos.reshape(N_EXPERTS, n_tok),
            bex.reshape(N_EXPERTS, nb))


COMBINE_WINDOW = 128
COMBINE_STRIDE = COMBINE_WINDOW - BF16_SUBLANES


def _ple_kernel(lo_ref, cnt_ref, x_ref, pos_ref, p_ref, np_ref, wg_ref, wp_ref, fn_ref, out_hbm, o_ref,
                wbuf, xbuf, y_scr, sem, sem_x, *, final, cap, n_tiles):
    i = pl.program_id(0)
    slot = i % 2
    tm = x_ref.shape[0]
    W = COMBINE_WINDOW
    n_rows = N_EXPERTS * cap

    def win_start(row):
        aligned = jnp.left_shift(jnp.right_shift(row, 4), 4)
        return pl.multiple_of(jnp.minimum(aligned, n_rows - W), BF16_SUBLANES)

    def issue(tile, sl):
        for e in range(N_EXPERTS):
            st = win_start(e * cap + lo_ref[tile * N_EXPERTS + e])
            pltpu.make_async_copy(out_hbm.at[pl.ds(st, W)], wbuf.at[sl, pl.ds(e * W, W)], sem.at[sl]).start()

    @pl.when(i == 0)
    def _():
        issue(0, 0)

    @pl.when(i + 1 < n_tiles)
    def _():
        issue(i + 1, 1 - slot)

    pltpu.make_async_copy(out_hbm.at[pl.ds(0, N_EXPERTS * W)], wbuf.at[slot], sem.at[slot]).wait()

    posv = pos_ref[...]
    lane = lax.broadcasted_iota(jnp.int32, (tm, W), 1).astype(F32)
    parts = []
    for e in range(N_EXPERTS):
        st = win_start(e * cap + lo_ref[i * N_EXPERTS + e])
        rel = posv[:, e:e + 1] - (st - e * cap).astype(F32)
        parts.append((rel == lane).astype(BF16))
    y_scr[...] = jnp.dot(jnp.concatenate(parts, axis=1), wbuf[slot], preferred_element_type=F32)

    n_extra = []
    for e in range(N_EXPERTS):
        first = lo_ref[i * N_EXPERTS + e]
        covered_end = win_start(e * cap + first) - e * cap + W
        remaining = jnp.maximum(first + cnt_ref[i * N_EXPERTS + e] - covered_end, 0)
        n_extra.append((covered_end, (remaining + COMBINE_STRIDE - 1) // COMBINE_STRIDE))

    @pl.when(functools.reduce(lambda a, b: a + b, [n for _, n in n_extra]) > 0)
    def _():
        for e in range(N_EXPERTS):
            covered_end, n_e = n_extra[e]

            def extra(k, carry, e=e, covered_end=covered_end):
                want = covered_end + k * COMBINE_STRIDE
                st = win_start(e * cap + want)
                cp = pltpu.make_async_copy(out_hbm.at[pl.ds(st, W)], xbuf, sem_x)
                cp.start()
                cp.wait()
                pe = pos_ref[:, e:e + 1]
                hit = (pe - (st - e * cap).astype(F32) == lane) & (pe >= want.astype(F32)) \
                    & (pe < (want + COMBINE_STRIDE).astype(F32))
                y_scr[...] += jnp.dot(hit.astype(BF16), xbuf[...], preferred_element_type=F32)
                return carry

            lax.fori_loop(0, n_e, extra, 0)

    x = x_ref[...] + y_scr[...]
    ms = jnp.mean(x * x, axis=-1, keepdims=True)
    h = (x * lax.rsqrt(ms + EPS) * np_ref[...]).astype(BF16)
    gate = _sigmoid(jnp.dot(h, wg_ref[...], preferred_element_type=F32))
    pp = jnp.dot(p_ref[...].astype(BF16), wp_ref[...], preferred_element_type=F32)
    x = x + gate * pp
    if final:
        ms = jnp.mean(x * x, axis=-1, keepdims=True)
        x = x * lax.rsqrt(ms + EPS) * fn_ref[...]
    o_ref[...] = x


def _ple(x2d, out, pos, bex, p_all, layer, norm_ple, wg, wp, final_norm, final):
    T = x2d.shape[0]
    tm = min(TM_PLE, T)
    n_tiles = T // tm
    p_off = layer * n_tiles
    cap = out.shape[0] // N_EXPERTS
    W = COMBINE_WINDOW
    assert cap >= W and cap % BF16_SUBLANES == 0 and tm % LANES == 0
    lo = bex[:, ::tm // LANES].astype(jnp.int32)
    cnt = jnp.diff(lo, axis=1, append=jnp.full((N_EXPERTS, 1), cap, jnp.int32))
    lo = lo.T.reshape(-1)
    cnt = cnt.T.reshape(-1)
    pos_t = pos.T
    vmem = (2 * (2 * tm * D_MODEL * 4 + tm * LANES * 4 + tm * PLE_DIM * 4 + D_MODEL * D_MODEL * 2
                 + PLE_DIM * D_MODEL * 2) + 2 * N_EXPERTS * W * D_MODEL * 2 + W * D_MODEL * 2
            + tm * N_EXPERTS * W * 2 + 6 * tm * D_MODEL * 4)
    tok = lambda width: pl.BlockSpec((tm, width), lambda i, lo, cnt: (i, 0))
    full = lambda shape: pl.BlockSpec(shape, lambda i, lo, cnt: (0,) * len(shape))
    grid_spec = pltpu.PrefetchScalarGridSpec(
        num_scalar_prefetch=2,
        grid=(n_tiles,),
        in_specs=[tok(D_MODEL), tok(N_EXPERTS),
                  pl.BlockSpec((tm, PLE_DIM), lambda i, lo, cnt: (p_off + i, 0)),
                  full((1, D_MODEL)), full((D_MODEL, D_MODEL)),
                  full((PLE_DIM, D_MODEL)), full((1, D_MODEL)), pl.BlockSpec(memory_space=pl.ANY)],
        out_specs=tok(D_MODEL),
        scratch_shapes=[
            pltpu.VMEM((2, N_EXPERTS * W, D_MODEL), BF16),
            pltpu.VMEM((W, D_MODEL), BF16),
            pltpu.VMEM((tm, D_MODEL), F32),
            pltpu.SemaphoreType.DMA((2,)),
            pltpu.SemaphoreType.DMA,
        ],
    )
    return pl.pallas_call(
        functools.partial(_ple_kernel, final=final, cap=cap, n_tiles=n_tiles),
        grid_spec=grid_spec,
        out_shape=jax.ShapeDtypeStruct((T, D_MODEL), F32),
        compiler_params=pltpu.CompilerParams(
            dimension_semantics=("arbitrary",), vmem_limit_bytes=_vmem_limit(vmem)),
        name="combine_ple",
    )(lo, cnt, x2d, pos_t, p_all, norm_ple, wg, wp, final_norm, out)


def _prep_layer(i, norm_mix, w_in, conv_w, na_rpb, gla_w2, gla_b, gla_norm, w_branch, w_out, norm_ffn, w_router,
                w_exp_gate, w_exp_up, w_exp_down, norm_ple, w_ple_gate, w_ple_proj):
    w = w_in[i]
    sizes = (CONV_WIDTH,) * 3 + (NA_WIDTH,) * 3 + (GLA_KEY_WIDTH,) * 2 + (GLA_VAL_WIDTH,) * 2 + (2 * GLA_RANK,)
    offs = np.concatenate([[0], np.cumsum(sizes)])
    gates_off = int(offs[-1])
    lr_off = int(offs[10])
    w_main = jnp.concatenate([w[:, gates_off:], w[:, :lr_off]], axis=1)
    assert float(np.log2(GLA_DK ** -0.5)).is_integer()
    col_scale = np.ones((PROJ_COLS,), np.float32)
    col_scale[COL_NQ:COL_NQ + NA_WIDTH] = NA_HEAD_DIM ** -0.5 * LOG2E
    col_scale[COL_GQ:COL_GQ + GLA_KEY_WIDTH] = GLA_DK ** -0.5
    w_main = (w_main * col_scale).astype(BF16)
    w_lr = jnp.zeros((D_MODEL, LANES), F32).at[:, :2 * GLA_RANK].set(w[:, lr_off:gates_off]).astype(BF16)
    return dict(
        norm_mix=norm_mix[i].reshape(1, D_MODEL), w_main=w_main, w_lr=w_lr, conv_w=conv_w[i], na_rpb=na_rpb[i],
        gla_w2=gla_w2[i], gla_b=gla_b[i], gla_norm=gla_norm[i], w_branch=w_branch[i].astype(BF16),
        w_out=w_out[i].astype(BF16), norm_ffn=norm_ffn[i].reshape(1, D_MODEL), w_router=w_router[i],
        w_exp_gate=w_exp_gate[i].astype(BF16), w_exp_up=w_exp_up[i].astype(BF16),
        w_exp_down=w_exp_down[i].astype(BF16), norm_ple=norm_ple[i].reshape(1, D_MODEL),
        w_ple_gate=w_ple_gate[i].astype(BF16), w_ple_proj=w_ple_proj[i].astype(BF16))


def _moe(hp, aff, lp):
    n_tok = hp.shape[0]
    cap = CAPACITY_FACTOR * n_tok // N_EXPERTS
    idx, gate_val, pos, bex = _route(aff, cap)
    out = _moe_ffn(hp, idx, gate_val, lp["w_exp_gate"], lp["w_exp_up"], lp["w_exp_down"])
    return out, pos, bex


def _trunk(x, p, layers, final_norm):
    bsz, L, _ = x.shape
    T = bsz * L
    x2d = x.reshape(T, D_MODEL)
    fn = final_norm.reshape(1, D_MODEL)
    for i, lp in enumerate(layers):
        proj, lr = _inproj(x2d, lp["norm_mix"], lp["w_main"], lp["w_lr"])
        y_b = _na_attention(proj, lp["na_rpb"], bsz, L)
        y_c = _gla(proj, lr, lp["gla_w2"], lp["gla_b"], lp["gla_norm"], bsz, L)
        x2d, h, aff = _merge(x2d, proj, y_b, y_c, lp["conv_w"], lp["w_branch"], lp["w_out"], lp["norm_ffn"],
                             lp["w_router"], L)
        out, pos, bex = _moe(h, aff, lp)
        x2d = _ple(x2d, out, pos, bex, p.reshape(-1, PLE_DIM), i, lp["norm_ple"], lp["w_ple_gate"], lp["w_ple_proj"], fn,
                   final=(i == len(layers) - 1))
    return x2d.reshape(bsz, L, D_MODEL)


def kernel(x_prompt, x_sample, p_prompt, p_sample, norm_mix, w_in, conv_w, na_rpb, gla_w2, gla_b, gla_norm, w_branch, w_out, norm_ffn, w_router, w_exp_gate, w_exp_up, w_exp_down, norm_ple, w_ple_gate, w_ple_proj, final_norm):
    depth = w_in.shape[0]
    layers = [_prep_layer(i, norm_mix, w_in, conv_w, na_rpb, gla_w2, gla_b, gla_norm, w_branch, w_out, norm_ffn,
                          w_router, w_exp_gate, w_exp_up, w_exp_down, norm_ple, w_ple_gate, w_ple_proj)
              for i in range(depth)]
    y_prompt = _trunk(x_prompt, p_prompt, layers, final_norm)
    y_sample = _trunk(x_sample, p_sample, layers, final_norm)
    return (y_prompt, y_sample)
```
